```python
import jax, jax.numpy as jnp
from jax import lax
import numpy as np

D_MODEL = 1024
BATCH = 4
SEQ = 8192
DEPTH = 4

PLE_DIM = 256
N_MIXERS = 3
EPS = 1e-6
SB_HEADS = 16
SB_HEAD_DIM = D_MODEL // SB_HEADS
Q_BLOCK = 128
POOL_WINDOWS = (2, 4, 8, 16)
POOL_GROUPS = len(POOL_WINDOWS)
POOL_GROUP_DIM = D_MODEL // POOL_GROUPS
HGRN_EXPAND = 128
HGRN_HEADS = D_MODEL // HGRN_EXPAND
HGRN_CHUNK = 128
D_FF = 256 * (-(-8 * D_MODEL // (3 * 256)))
N_EXPERTS = 8
TOP_K = 2
D_FF_EXPERT = D_FF // 2
N_SB = (DEPTH + 2) // 3
N_POOL = (DEPTH + 1) // 3
N_HGRN = DEPTH // 3
N_DENSE = (DEPTH + 1) // 2
N_MOE = DEPTH // 2

kernel_name = "hybrid_stickbreak_pool_hgrn2_moe"


def rmsnorm(x, g):
    xf = x.astype(jnp.float32)
    y = xf * lax.rsqrt(jnp.mean(xf * xf, axis=-1, keepdims=True) + EPS)
    return (y * g.astype(jnp.float32)).astype(x.dtype)


def stick_breaking_attention(xn, w_qkv, q_gain, k_gain, w_o):
    B, S, _ = xn.shape
    qkv = (xn @ w_qkv).astype(jnp.float32)
    q, k, v = jnp.split(qkv, 3, axis=-1)
    heads = lambda t: t.reshape(B, S, SB_HEADS, SB_HEAD_DIM).transpose(0, 2, 1, 3)
    q = rmsnorm(heads(q), q_gain).astype(jnp.float32)
    k = rmsnorm(heads(k), k_gain).astype(jnp.float32)
    v = heads(v)
    scale = SB_HEAD_DIM ** -0.5
    outs = []
    for blk in range(S // Q_BLOCK):
        t0 = blk * Q_BLOCK
        L = t0 + Q_BLOCK
        z = jnp.einsum('bhqd,bhkd->bhqk', q[:, :, t0:L], k[:, :, :L]) * scale
        t_pos = t0 + jnp.arange(Q_BLOCK)[:, None]
        s_pos = jnp.arange(L)[None, :]
        mask = s_pos < t_pos
        log_keep = jnp.where(mask, jax.nn.log_sigmoid(-z), 0.0)
        log_after = lax.cumsum(log_keep, axis=3, reverse=True) - log_keep
        log_a = jnp.where(mask, jax.nn.log_sigmoid(z) + log_after, -jnp.inf)
        a = jnp.exp(log_a)
        outs.append(jnp.einsum('bhqk,bhkd->bhqd', a, v[:, :, :L]))
    o = jnp.concatenate(outs, axis=2).transpose(0, 2, 1, 3).reshape(B, S, D_MODEL)
    return o.astype(xn.dtype) @ w_o


def multiscale_pool_mixer(xn, w_in, w_grp, chan_scale):
    B, S, _ = xn.shape
    u = (xn @ w_in).astype(jnp.float32).reshape(B, S, POOL_GROUPS, POOL_GROUP_DIM)
    cs = jnp.cumsum(u, axis=1)
    outs = []
    for g, w in enumerate(POOL_WINDOWS):
        c = cs[:, :, g]
        prev = jnp.pad(c, ((0, 0), (w, 0), (0, 0)))[:, :S]
        cnt = jnp.minimum(jnp.arange(S) + 1, w).astype(jnp.float32)[None, :, None]
        outs.append((c - prev) / cnt - u[:, :, g])
    pooled = jnp.stack(outs, axis=2)
    y = jnp.einsum('bsgc,gcd->bsgd', pooled, w_grp.astype(jnp.float32)).reshape(B, S, D_MODEL)
    return (y * chan_scale.astype(jnp.float32)).astype(xn.dtype)


def hgrn2_mixer(xn, w_in, lower_bounds, layer_idx, o_gain, w_o):
    B, S, _ = xn.shape
    H, DK, C = HGRN_HEADS, HGRN_EXPAND, HGRN_CHUNK
    N = S // C
    proj = (xn @ w_in).astype(jnp.float32)
    q, fg, i_in, g = jnp.split(proj, 4, axis=-1)
    s = jax.nn.softmax(lower_bounds.astype(jnp.float32), axis=0)
    lb = (jnp.cumsum(s, axis=0) - s[0])[layer_idx]
    log_f = jnp.logaddexp(jnp.log(lb), jnp.log1p(-lb) + jax.nn.log_sigmoid(fg))
    k = (1.0 - lb) * jax.nn.sigmoid(-fg)
    q = jax.nn.silu(q)

    def to_chunks(t):
        return t.reshape(B, N, C, H, DK).transpose(1, 0, 3, 2, 4)

    causal = jnp.tril(jnp.ones((C, C), dtype=bool))

    def step(state, inp):
        qc, kc, vc, lfc = inp
        b = jnp.cumsum(lfc, axis=2)
        o_inter = jnp.einsum('bhtk,bhkv->bhtv', qc * jnp.exp(b), state)
        diff = b[:, :, :, None, :] - b[:, :, None, :, :]
        decay = jnp.exp(jnp.where(causal[:, :, None], diff, -jnp.inf))
        scores = jnp.einsum('bhtk,bhtsk,bhsk->bhts', qc, decay, kc)
        o_intra = jnp.einsum('bhts,bhsv->bhtv', scores, vc)
        b_end = b[:, :, -1, :]
        new_state = state * jnp.exp(b_end)[..., None] + jnp.einsum(
            'bhsk,bhsv->bhkv', kc * jnp.exp(b_end[:, :, None, :] - b), vc)
        return new_state, o_inter + o_intra

    state0 = jnp.zeros((B, H, DK, DK), jnp.float32)
    _, o = lax.scan(step, state0, (to_chunks(q), to_chunks(k), to_chunks(i_in), to_chunks(log_f)))
    o = o.transpose(1, 0, 3, 2, 4).reshape(B, S, H, DK)
    o = rmsnorm(o, o_gain) * jax.nn.silu(g).reshape(B, S, H, DK)
    return o.reshape(B, S, D_MODEL).astype(xn.dtype) @ w_o


def swiglu(hn, w_gu, w_d):
    gate, up = jnp.split(hn @ w_gu, 2, axis=-1)
    return (jax.nn.silu(gate) * up) @ w_d


def moe_swiglu(hn, w_router, w_gu, w_d):
    logits = (hn @ w_router).astype(jnp.float32)
    top_vals, top_idx = lax.top_k(logits, TOP_K)
    gates = jax.nn.softmax(top_vals, axis=-1)
    combine = jnp.sum(jax.nn.one_hot(top_idx, N_EXPERTS, dtype=jnp.float32) * gates[..., None], axis=-2)
    out = jnp.zeros(hn.shape, jnp.float32)
    for e in range(N_EXPERTS):
        out = out + combine[..., e:e + 1] * swiglu(hn, w_gu[e], w_d[e]).astype(jnp.float32)
    return out.astype(hn.dtype)


def setup_inputs(seed: int = 0) -> dict:
    key = jax.random.key(seed)
    ks = iter(jax.random.split(key, 32))
    f32 = jnp.float32

    def w(shape, fan_in):
        return jax.random.normal(next(ks), shape, f32) * fan_in ** -0.5

    def gain(shape):
        return 1.0 + 0.02 * jax.random.normal(next(ks), shape, f32)

    return {
        "x": jax.random.normal(next(ks), (BATCH, SEQ, D_MODEL), f32),
        "p": jax.random.normal(next(ks), (DEPTH, BATCH, SEQ, PLE_DIM), f32),
        "norm_mix": gain((DEPTH, D_MODEL)),
        "norm_ffn": gain((DEPTH, D_MODEL)),
        "sb_w_qkv": w((N_SB, D_MODEL, 3 * D_MODEL), D_MODEL),
        "sb_q_gain": gain((N_SB, SB_HEAD_DIM)),
        "sb_k_gain": gain((N_SB, SB_HEAD_DIM)),
        "sb_w_o": w((N_SB, D_MODEL, D_MODEL), D_MODEL),
        "pool_w_in": w((N_POOL, D_MODEL, D_MODEL), D_MODEL),
        "pool_w_grp": w((N_POOL, POOL_GROUPS, POOL_GROUP_DIM, POOL_GROUP_DIM), POOL_GROUP_DIM),
        "pool_scale": gain((N_POOL, D_MODEL)),
        "hgrn_w_in": w((N_HGRN, D_MODEL, 4 * D_MODEL), D_MODEL),
        "hgrn_lower_bounds": 0.1 * jax.random.normal(next(ks), (DEPTH, D_MODEL), f32),
        "hgrn_o_gain": gain((N_HGRN, HGRN_EXPAND)),
        "hgrn_w_o": w((N_HGRN, D_MODEL, D_MODEL), D_MODEL),
        "ffn_w_gu": w((N_DENSE, D_MODEL, 2 * D_FF), D_MODEL),
        "ffn_w_d": w((N_DENSE, D_FF, D_MODEL), D_FF),
        "moe_router": w((N_MOE, D_MODEL, N_EXPERTS), D_MODEL),
        "moe_w_gu": w((N_MOE, N_EXPERTS, D_MODEL, 2 * D_FF_EXPERT), D_MODEL),
        "moe_w_d": w((N_MOE, N_EXPERTS, D_FF_EXPERT, D_MODEL), D_FF_EXPERT),
        "ple_w": w((DEPTH, PLE_DIM, D_MODEL), PLE_DIM),
        "ple_gate_norm": gain((DEPTH, D_MODEL)),
        "ple_gate_w": w((DEPTH, D_MODEL, D_MODEL), D_MODEL),
    }


def reference(x, p, norm_mix, norm_ffn, sb_w_qkv, sb_q_gain, sb_k_gain, sb_w_o,
              pool_w_in, pool_w_grp, pool_scale, hgrn_w_in, hgrn_lower_bounds,
              hgrn_o_gain, hgrn_w_o, ffn_w_gu, ffn_w_d, moe_router, moe_w_gu, moe_w_d,
              ple_w, ple_gate_norm, ple_gate_w):
    h = x
    for i in range(DEPTH):
        xn = rmsnorm(h, norm_mix[i])
        kind, j = i % N_MIXERS, i // N_MIXERS
        if kind == 0:
            y = stick_breaking_attention(xn, sb_w_qkv[j], sb_q_gain[j], sb_k_gain[j], sb_w_o[j])
        elif kind == 1:
            y = multiscale_pool_mixer(xn, pool_w_in[j], pool_w_grp[j], pool_scale[j])
        else:
            y = hgrn2_mixer(xn, hgrn_w_in[j], hgrn_lower_bounds, i, hgrn_o_gain[j], hgrn_w_o[j])
        h = h + y.astype(h.dtype)
        hn = rmsnorm(h, norm_ffn[i])
        c = i // 2
        if i % 2 == 0:
            h = h + swiglu(hn, ffn_w_gu[c], ffn_w_d[c]).astype(h.dtype)
        else:
            h = h + moe_swiglu(hn, moe_router[c], moe_w_gu[c], moe_w_d[c]).astype(h.dtype)
        gate = jax.nn.sigmoid(rmsnorm(h, ple_gate_norm[i]) @ ple_gate_w[i])
        h = h + (gate * (p[i] @ ple_w[i])).astype(h.dtype)
    return h
```

```python
import functools

import jax
import jax.numpy as jnp
from jax import lax
from jax.experimental import pallas as pl
from jax.experimental.pallas import tpu as pltpu

F32 = jnp.float32
BF16 = jnp.bfloat16

EPS = 1e-6
N_MIXERS = 3
SB_HEAD_DIM = 64
POOL_WINDOWS = (2, 4, 8, 16)
HGRN_CHUNK = 128
TOP_K = 2

LANES = 128
POOL_HALO = 16
VMEM_LIMIT = 56 * 1024 * 1024
LOG_ZERO_F32 = -104.0


def _dot(a, b):
    return jnp.dot(a, b, preferred_element_type=F32)


def _dot_nt(a, b):
    return lax.dot_general(a, b, (((1,), (1,)), ((), ())), preferred_element_type=F32)


def _split_bf16(x):
    hi = x.astype(BF16)
    lo = (x - hi.astype(F32)).astype(BF16)
    return hi, lo


def _rms(x, g):
    ms = jnp.mean(x * x, axis=-1, keepdims=True)
    return x * lax.rsqrt(ms + EPS) * g


def _log_sigmoid(x):
    return -(jnp.maximum(-x, 0.0) + jnp.log(1.0 + jnp.exp(-jnp.abs(x))))


def _silu(x):
    return x * jax.nn.sigmoid(x)


def _params(n_axes):
    return pltpu.CompilerParams(
        dimension_semantics=("arbitrary",) * n_axes, vmem_limit_bytes=VMEM_LIMIT)


def _resident(shape):
    nd = len(shape)
    return pl.BlockSpec(shape, lambda *_: (0,) * nd)


def _norm_matmul_kernel(h_ref, g_ref, w_ref, o_ref, *, n_chunk):
    xn = _rms(h_ref[...], g_ref[...]).astype(BF16)
    for c0 in range(0, w_ref.shape[1], n_chunk):
        o_ref[:, c0:c0 + n_chunk] = _dot(xn, w_ref[:, c0:c0 + n_chunk]).astype(o_ref.dtype)


def _norm_matmul(h, g, w, out_dtype, tm, name):
    t, d = h.shape
    n = w.shape[1]
    return pl.pallas_call(
        functools.partial(_norm_matmul_kernel, n_chunk=512),
        grid=(t // tm,),
        in_specs=[pl.BlockSpec((tm, d), lambda i: (i, 0)), _resident((1, d)), _resident((d, n))],
        out_specs=pl.BlockSpec((tm, n), lambda i: (i, 0)),
        out_shape=jax.ShapeDtypeStruct((t, n), out_dtype),
        compiler_params=_params(1),
        name=name,
    )(h, g, w)


def _sb_attn_kernel(q_ref, k_ref, v_ref, qg_ref, kg_ref, o_ref,
                    qn_s, kcat_s, vcat_s, oacc_s, carry_s, *, blk):
    s_len = q_ref.shape[1]
    n_blk = s_len // blk
    scale = SB_HEAD_DIM ** -0.5

    row = lax.broadcasted_iota(jnp.int32, (blk, LANES), 0)
    col = lax.broadcasted_iota(jnp.int32, (blk, LANES), 1)
    head_sum = jnp.where((row >> 6) == (col >> 6), 1.0, 0.0).astype(BF16)
    head0 = col < SB_HEAD_DIM

    def head_norm(x, gain):
        hi, lo = _split_bf16(x * x)
        ss = _dot(hi, head_sum) + _dot(lo, head_sum)
        return x * lax.rsqrt(ss * (1.0 / SB_HEAD_DIM) + EPS) * gain

    def prep(c, _):
        r = pl.ds(pl.multiple_of(c * blk, blk), blk)
        qn = head_norm(q_ref[0, r, :].astype(F32), qg_ref[...] * scale)
        qn_s[r, :] = qn.astype(BF16)
        kn = head_norm(k_ref[0, r, :].astype(F32), kg_ref[...]).astype(BF16)
        vv = v_ref[0, r, :]
        zero = jnp.zeros_like(kn)
        kcat_s[c] = jnp.concatenate([jnp.where(head0, kn, zero), jnp.where(head0, zero, kn)], axis=0)
        vcat_s[c] = jnp.concatenate([jnp.where(head0, vv, zero), jnp.where(head0, zero, vv)], axis=0)
        return 0

    lax.fori_loop(0, n_blk, prep, 0)

    r2 = lax.broadcasted_iota(jnp.int32, (2 * blk, 2 * blk), 0) & (blk - 1)
    c2 = lax.broadcasted_iota(jnp.int32, (2 * blk, 2 * blk), 1)
    cum_mat = jnp.where((c2 >= blk) | (r2 > c2), 1.0, 0.0).astype(BF16)

    row2 = lax.broadcasted_iota(jnp.int32, (blk, 2 * blk), 0)
    col2 = lax.broadcasted_iota(jnp.int32, (blk, 2 * blk), 1) & (blk - 1)

    def q_block(qb, _):
        r = pl.ds(pl.multiple_of(qb * blk, blk), blk)
        q_blk = qn_s[r, :]
        oacc_s[...] = jnp.zeros_like(oacc_s)
        carry_s[...] = jnp.zeros_like(carry_s)

        def kv_cond(st):
            kb, top = st
            return jnp.logical_and(kb >= 0, top > LOG_ZERO_F32)

        def kv_step(st):
            kb, _ = st
            z = _dot_nt(q_blk, kcat_s[kb])
            valid = (col2 + kb * blk) < (row2 + qb * blk)
            log_keep = -(jnp.maximum(z, 0.0) + jnp.log(1.0 + jnp.exp(-jnp.abs(z))))
            log_keep = jnp.where(valid, log_keep, 0.0)
            a_heads = []
            for hd in range(2):
                cols = slice(hd * blk, (hd + 1) * blk)
                lk = log_keep[:, cols]
                hi, lo = _split_bf16(lk)
                cs = _dot(jnp.concatenate([hi, lo], axis=1), cum_mat)
                carry = carry_s[hd]
                log_a = z[:, cols] + lk + cs[:, :blk] + carry
                a_heads.append(jnp.where(valid[:, cols], jnp.exp(log_a), 0.0).astype(BF16))
                carry_s[hd] = carry + cs[:, blk:]
            oacc_s[...] += _dot(jnp.concatenate(a_heads, axis=1), vcat_s[kb])
            top = jnp.max(jnp.maximum(carry_s[0], carry_s[1]))
            return kb - 1, top

        lax.while_loop(kv_cond, kv_step, (qb, jnp.float32(0.0)))
        o_ref[0, r, :] = oacc_s[...].astype(o_ref.dtype)
        return 0

    lax.fori_loop(0, n_blk, q_block, 0)


def _sb_attention(qkv, q_gain, k_gain, name):
    b, s, d3 = qkv.shape
    d = d3 // 3
    n_pair = d // LANES
    blk = 128
    n_blk = s // blk
    gains = [jnp.tile(g.astype(F32), LANES // SB_HEAD_DIM).reshape(1, LANES) for g in (q_gain, k_gain)]
    col_block = lambda off: pl.BlockSpec((1, s, LANES), lambda bi, hp: (bi, 0, off + hp))
    return pl.pallas_call(
        functools.partial(_sb_attn_kernel, blk=blk),
        grid=(b, n_pair),
        in_specs=[col_block(0), col_block(n_pair), col_block(2 * n_pair),
                  _resident((1, LANES)), _resident((1, LANES))],
        out_specs=pl.BlockSpec((1, s, LANES), lambda bi, hp: (bi, 0, hp)),
        out_shape=jax.ShapeDtypeStruct((b, s, d), BF16),
        scratch_shapes=[
            pltpu.VMEM((s, LANES), BF16),
            pltpu.VMEM((n_blk, 2 * blk, LANES), BF16),
            pltpu.VMEM((n_blk, 2 * blk, LANES), BF16),
            pltpu.VMEM((blk, LANES), F32),
            pltpu.VMEM((2, blk, LANES), F32),
        ],
        compiler_params=_params(2),
        name=name,
    )(qkv, qkv, qkv, *gains)


def _pool_kernel(h_ref, g_ref, win_ref, wgrp_ref, sc_ref, o_ref, ext_s, *, tm, seq):
    i = pl.program_id(0)
    t0 = (i * tm) % seq

    @pl.when(t0 == 0)
    def _():
        ext_s[0:POOL_HALO, :] = jnp.zeros((POOL_HALO, ext_s.shape[1]), F32)

    x = h_ref[...]
    u = _dot(_rms(x, g_ref[...]).astype(BF16), win_ref[...])
    ext_s[POOL_HALO:POOL_HALO + tm, :] = u

    pos = t0 + lax.broadcasted_iota(jnp.int32, (tm, 1), 0)
    gdim = u.shape[1] // len(POOL_WINDOWS)
    for g, w in enumerate(POOL_WINDOWS):
        cols = slice(g * gdim, (g + 1) * gdim)
        ug = u[:, cols]
        win_sum = ug
        for back in range(1, w):
            win_sum = win_sum + ext_s[POOL_HALO - back:POOL_HALO - back + tm, cols]
        cnt = jnp.minimum(pos + 1, w).astype(F32)
        pooled = win_sum / cnt - ug
        y = _dot(pooled.astype(BF16), wgrp_ref[g]) * sc_ref[:, cols]
        o_ref[:, cols] = x[:, cols] + y

    ext_s[0:POOL_HALO, :] = ext_s[tm:tm + POOL_HALO, :]


def _pool_mixer(h, seq, g, w_in, w_grp, scale, tm, name):
    t, d = h.shape
    ng, gd, _ = w_grp.shape
    return pl.pallas_call(
        functools.partial(_pool_kernel, tm=tm, seq=seq),
        grid=(t // tm,),
        in_specs=[pl.BlockSpec((tm, d), lambda i: (i, 0)), _resident((1, d)), _resident((d, d)),
                  _resident((ng, gd, gd)), _resident((1, d))],
        out_specs=pl.BlockSpec((tm, d), lambda i: (i, 0)),
        out_shape=jax.ShapeDtypeStruct((t, d), F32),
        scratch_shapes=[pltpu.VMEM((POOL_HALO + tm, d), F32)],
        compiler_params=_params(1),
        name=name,
    )(h, g, w_in, w_grp, scale)


_HGRN_LEVELS = (64, 32, 16, 8, 4, 2, 1)


def _hgrn_kernel(q_ref, f_ref, i_ref, g_ref, lbp_ref, og_ref, o_ref, cum_s, state_s, *, layer_idx):
    c = HGRN_CHUNK
    n_chunk = q_ref.shape[1] // c

    lbp = lbp_ref[...]
    e = jnp.exp(lbp - jnp.max(lbp, axis=0, keepdims=True))
    sm = e / jnp.sum(e, axis=0, keepdims=True)
    lb = jnp.sum(sm[1:layer_idx + 1], axis=0, keepdims=True)
    log_lb = jnp.log(lb)
    log_1m_lb = jnp.log1p(-lb)

    r = lax.broadcasted_iota(jnp.int32, (c, c), 0)
    s = lax.broadcasted_iota(jnp.int32, (c, c), 1)
    one = lambda m: jnp.where(m, 1.0, 0.0).astype(BF16)
    cum_s[0:c, :] = one(s <= r)
    cum_s[c:2 * c, :] = one(s > r)
    for l, m in enumerate(_HGRN_LEVELS):
        sh = m.bit_length() - 1
        same = (r >> sh) == (s >> sh)
        odd = ((r >> sh) & 1) == 1
        cum_s[(2 + l) * c:(3 + l) * c, :] = one(same & (odd == (s <= r)))

    state_s[...] = jnp.zeros_like(state_s)

    def chunk(n, _):
        rows = pl.ds(pl.multiple_of(n * c, c), c)
        fg = f_ref[0, rows, :].astype(F32)
        a_ = log_lb
        c_ = log_1m_lb + _log_sigmoid(fg)
        log_f = jnp.maximum(a_, c_) + jnp.log(1.0 + jnp.exp(-jnp.abs(a_ - c_)))
        k = (1.0 - lb) * jax.nn.sigmoid(-fg)
        q = _silu(q_ref[0, rows, :].astype(F32))
        v = i_ref[0, rows, :]

        hi, lo = _split_bf16(log_f)
        cum = _dot(cum_s[...], hi) + _dot(cum_s[...], lo)
        dec = jnp.exp(cum)
        b_end = dec[c - 1:c, :]

        state = state_s[...]
        o = _dot_nt((q * dec[0:c]).astype(BF16), state.astype(BF16))
        kd = (k * dec[c:2 * c]).astype(BF16)
        v_t = v.astype(F32).T.astype(BF16)
        state_s[...] = state * b_end + _dot(v_t, kd)

        scores = jnp.zeros((c, c), F32)
        for l, m in enumerate(_HGRN_LEVELS):
            sh = m.bit_length() - 1
            d_l = dec[(2 + l) * c:(3 + l) * c]
            sc = _dot_nt((q * d_l).astype(BF16), (k * d_l).astype(BF16))
            pair = (((r >> sh) & 1) == 1) & ((s >> sh) == (r >> sh) - 1)
            scores = scores + jnp.where(pair, sc, 0.0)
        o = o + _dot(scores.astype(BF16), v)
        o = o + jnp.sum(q * k, axis=-1, keepdims=True) * v.astype(F32)

        o = _rms(o, og_ref[...]) * _silu(g_ref[0, rows, :].astype(F32))
        o_ref[0, rows, :] = o.astype(o_ref.dtype)
        return 0

    lax.fori_loop(0, n_chunk, chunk, 0)


def _hgrn_core(proj, lower_bounds, layer_idx, o_gain, name):
    b, s, d4 = proj.shape
    d = d4 // 4
    n_head = d // LANES
    depth = lower_bounds.shape[0]
    col_block = lambda off: pl.BlockSpec((1, s, LANES), lambda bi, hd: (bi, 0, off + hd))
    return pl.pallas_call(
        functools.partial(_hgrn_kernel, layer_idx=layer_idx),
        grid=(b, n_head),
        in_specs=[col_block(0), col_block(n_head), col_block(2 * n_head), col_block(3 * n_head),
                  pl.BlockSpec((depth, LANES), lambda bi, hd: (0, hd)), _resident((1, LANES))],
        out_specs=pl.BlockSpec((1, s, LANES), lambda bi, hd: (bi, 0, hd)),
        out_shape=jax.ShapeDtypeStruct((b, s, d), BF16),
        scratch_shapes=[pltpu.VMEM(((2 + len(_HGRN_LEVELS)) * HGRN_CHUNK, HGRN_CHUNK), BF16),
                        pltpu.VMEM((LANES, LANES), F32)],
        compiler_params=_params(2),
        name=name,
    )(proj, proj, proj, proj, lower_bounds, o_gain.reshape(1, LANES))


def _swiglu_into(acc_ref, hn, wgu, wd, d_ff, ck, row_scale=None):
    for c0 in range(0, d_ff, ck):
        c1 = min(c0 + ck, d_ff)
        gate = _dot(hn, wgu[:, c0:c1])
        up = _dot(hn, wgu[:, d_ff + c0:d_ff + c1])
        act = _silu(gate) * up
        if row_scale is not None:
            act = act * row_scale
        acc_ref[...] += _dot(act.astype(BF16), wd[c0:c1, :])


def _ple(h, p, gn, wg, wp):
    gate = jax.nn.sigmoid(_dot(_rms(h, gn).astype(BF16), wg))
    return h + gate * _dot(p.astype(BF16), wp)


def _dense_post_kernel(h_ref, y_ref, p_ref, wo_ref, nf_ref, wgu_ref, wd_ref, gn_ref, wg_ref, wp_ref,
                       o_ref, acc_ref, *, ck):
    h1 = h_ref[...] + _dot(y_ref[...], wo_ref[...])
    acc_ref[...] = h1
    hn = _rms(h1, nf_ref[...]).astype(BF16)
    _swiglu_into(acc_ref, hn, wgu_ref, wd_ref, wd_ref.shape[0], ck)
    o_ref[...] = _ple(acc_ref[...], p_ref[...], gn_ref[...], wg_ref[...], wp_ref[...])


def _dense_post(h, y, p, wo, nf, wgu, wd, gn, wg, wp, tm, name):
    t, d = h.shape
    tile = lambda n: pl.BlockSpec((tm, n), lambda i: (i, 0))
    weights = (wo, nf, wgu, wd, gn, wg, wp)
    return pl.pallas_call(
        functools.partial(_dense_post_kernel, ck=256),
        grid=(t // tm,),
        in_specs=[tile(d), tile(d), tile(p.shape[1])] + [_resident(w.shape) for w in weights],
        out_specs=tile(d),
        out_shape=jax.ShapeDtypeStruct((t, d), F32),
        scratch_shapes=[pltpu.VMEM((tm, d), F32)],
        compiler_params=_params(1),
        name=name,
    )(h, y, p, *weights)


def _router(hn, wr_ref, n_exp):
    logits = _dot(hn, wr_ref[...])
    idx = lax.broadcasted_iota(jnp.int32, logits.shape, 1)
    m1 = jnp.max(logits, axis=-1, keepdims=True)
    i1 = jnp.min(jnp.where(logits == m1, idx, n_exp), axis=-1, keepdims=True)
    rest = jnp.where(idx == i1, -jnp.inf, logits)
    m2 = jnp.max(rest, axis=-1, keepdims=True)
    i2 = jnp.min(jnp.where(rest == m2, idx, n_exp), axis=-1, keepdims=True)
    e2 = jnp.exp(m2 - m1)
    g1 = 1.0 / (1.0 + e2)
    return jnp.where(idx == i1, g1, 0.0) + jnp.where(idx == i2, e2 * g1, 0.0)


def _route_proj_kernel(h_ref, y_ref, wo_ref, nf_ref, wr_ref, h1_ref, hn_ref, cmb_ref):
    h1 = h_ref[...] + _dot(y_ref[...], wo_ref[...])
    h1_ref[...] = h1
    hn = _rms(h1, nf_ref[...]).astype(BF16)
    hn_ref[...] = hn
    cmb_ref[...] = _router(hn, wr_ref, cmb_ref.shape[1])


def _route_kernel(h_ref, nf_ref, wr_ref, hn_ref, cmb_ref):
    hn = _rms(h_ref[...], nf_ref[...]).astype(BF16)
    hn_ref[...] = hn
    cmb_ref[...] = _router(hn, wr_ref, cmb_ref.shape[1])


def _route(h, y, wo, nf, wr, tm, name):
    t, d = h.shape
    n_exp = wr.shape[1]
    tile = lambda n: pl.BlockSpec((tm, n), lambda i: (i, 0))
    hn_cmb = [jax.ShapeDtypeStruct((t, d), BF16), jax.ShapeDtypeStruct((t, n_exp), F32)]
    if y is None:
        hn, cmb = pl.pallas_call(
            _route_kernel, grid=(t // tm,),
            in_specs=[tile(d), _resident(nf.shape), _resident(wr.shape)],
            out_specs=[tile(d), tile(n_exp)], out_shape=hn_cmb,
            compiler_params=_params(1), name=name,
        )(h, nf, wr)
        return h, hn, cmb
    return pl.pallas_call(
        _route_proj_kernel, grid=(t // tm,),
        in_specs=[tile(d), tile(d), _resident(wo.shape), _resident(nf.shape), _resident(wr.shape)],
        out_specs=[tile(d), tile(d), tile(n_exp)],
        out_shape=[jax.ShapeDtypeStruct((t, d), F32)] + hn_cmb,
        compiler_params=_params(1), name=name,
    )(h, y, wo, nf, wr)


def _moe_kernel(h1_ref, hn_ref, cmb_ref, p_ref, wgu_ref, wd_ref, gn_ref, wg_ref, wp_ref,
                o_ref, acc_ref, *, ck):
    e = pl.program_id(1)

    @pl.when(e == 0)
    def _():
        acc_ref[...] = h1_ref[...]

    cmb = cmb_ref[...]
    lane = lax.broadcasted_iota(jnp.int32, cmb.shape, 1)
    w_e = jnp.sum(jnp.where(lane == e, cmb, 0.0), axis=-1, keepdims=True)
    _swiglu_into(acc_ref, hn_ref[...], wgu_ref.at[0], wd_ref.at[0], wd_ref.shape[1], ck, row_scale=w_e)

    @pl.when(e == pl.num_programs(1) - 1)
    def _():
        o_ref[...] = _ple(acc_ref[...], p_ref[...], gn_ref[...], wg_ref[...], wp_ref[...])


def _moe_post(h1, hn, cmb, p, wgu, wd, gn, wg, wp, tm, name):
    t, d = h1.shape
    n_exp, d_ffe, _ = wd.shape
    tile = lambda n: pl.BlockSpec((tm, n), lambda i, e: (i, 0))
    return pl.pallas_call(
        functools.partial(_moe_kernel, ck=128 if d_ffe % 256 else 256),
        grid=(t // tm, n_exp),
        in_specs=[tile(d), tile(d), tile(n_exp), tile(p.shape[1]),
                  pl.BlockSpec((1, d, 2 * d_ffe), lambda i, e: (e, 0, 0)),
                  pl.BlockSpec((1, d_ffe, d), lambda i, e: (e, 0, 0)),
                  _resident(gn.shape), _resident(wg.shape), _resident(wp.shape)],
        out_specs=tile(d),
        out_shape=jax.ShapeDtypeStruct((t, d), F32),
        scratch_shapes=[pltpu.VMEM((tm, d), F32)],
        compiler_params=_params(2),
        name=name,
    )(h1, hn, cmb, p, wgu, wd, gn, wg, wp)


def kernel(x, p, norm_mix, norm_ffn, sb_w_qkv, sb_q_gain, sb_k_gain, sb_w_o, pool_w_in, pool_w_grp,
           pool_scale, hgrn_w_in, hgrn_lower_bounds, hgrn_o_gain, hgrn_w_o, ffn_w_gu, ffn_w_d,
           moe_router, moe_w_gu, moe_w_d, ple_w, ple_gate_norm, ple_gate_w):
    b, s, d = x.shape
    depth = p.shape[0]
    t = b * s
    tm = 512
    bf = lambda w: w.astype(BF16)
    vec = lambda g: g.reshape(1, -1)

    h = x.reshape(t, d)
    for i in range(depth):
        kind, j = i % N_MIXERS, i // N_MIXERS
        p_i = p[i].reshape(t, -1)
        ple = (vec(ple_gate_norm[i]), bf(ple_gate_w[i]), bf(ple_w[i]))
        y = w_o = None
        if kind == 0:
            qkv = _norm_matmul(h, vec(norm_mix[i]), bf(sb_w_qkv[j]), BF16, tm, f"sb_qkv_{i}")
            y = _sb_attention(qkv.reshape(b, s, 3 * d), sb_q_gain[j], sb_k_gain[j], f"sb_attn_{i}")
            y, w_o = y.reshape(t, d), bf(sb_w_o[j])
        elif kind == 1:
            h = _pool_mixer(h, s, vec(norm_mix[i]), bf(pool_w_in[j]), bf(pool_w_grp[j]),
                            vec(pool_scale[j]), tm, f"pool_{i}")
        else:
            proj = _norm_matmul(h, vec(norm_mix[i]), bf(hgrn_w_in[j]), BF16, tm, f"hgrn_in_{i}")
            y = _hgrn_core(proj.reshape(b, s, 4 * d), hgrn_lower_bounds, i, hgrn_o_gain[j], f"hgrn_{i}")
            y, w_o = y.reshape(t, d), bf(hgrn_w_o[j])

        c = i // 2
        if i % 2 == 0:
            h = _dense_post(h, y, p_i, w_o, vec(norm_ffn[i]), bf(ffn_w_gu[c]), bf(ffn_w_d[c]), *ple,
                            tm, f"dense_{i}")
        else:
            h1, hn, cmb = _route(h, y, w_o, vec(norm_ffn[i]), bf(moe_router[c]), tm, f"route_{i}")
            h = _moe_post(h1, hn, cmb, p_i, bf(moe_w_gu[c]), bf(moe_w_d[c]), *ple, 2 * tm, f"moe_{i}")
    return h.reshape(b, s, d)
```

```python
import functools

import jax
import jax.numpy as jnp
from jax import lax
from jax.experimental import pallas as pl
from jax.experimental.pallas import tpu as pltpu

F32 = jnp.float32
BF16 = jnp.bfloat16

EPS = 1e-6
N_MIXERS = 3
SB_HEAD_DIM = 64
POOL_WINDOWS = (2, 4, 8, 16)
HGRN_CHUNK = 128
TOP_K = 2

LANES = 128
SB_BLOCK = 256
POOL_HALO = 16
VMEM_LIMIT = 56 * 1024 * 1024
LOG_ZERO_F32 = -104.0


def _dot(a, b):
    return jnp.dot(a, b, preferred_element_type=F32)


def _dot_nt(a, b):
    return lax.dot_general(a, b, (((1,), (1,)), ((), ())), preferred_element_type=F32)


def _split_bf16(x):
    hi = x.astype(BF16)
    lo = (x - hi.astype(F32)).astype(BF16)
    return hi, lo


def _rms(x, g):
    ms = jnp.mean(x * x, axis=-1, keepdims=True)
    return x * lax.rsqrt(ms + EPS) * g


def _log_sigmoid(x):
    return -(jnp.maximum(-x, 0.0) + jnp.log(1.0 + jnp.exp(-jnp.abs(x))))


def _silu(x):
    return x * jax.nn.sigmoid(x)


def _params(n_axes):
    return pltpu.CompilerParams(
        dimension_semantics=("arbitrary",) * n_axes, vmem_limit_bytes=VMEM_LIMIT)


def _resident(shape):
    nd = len(shape)
    return pl.BlockSpec(shape, lambda *_: (0,) * nd)


def _norm_matmul_kernel(h_ref, g_ref, w_ref, o_ref, *, n_chunk):
    xn = _rms(h_ref[...], g_ref[...]).astype(BF16)
    for c0 in range(0, w_ref.shape[1], n_chunk):
        o_ref[:, c0:c0 + n_chunk] = _dot(xn, w_ref[:, c0:c0 + n_chunk]).astype(o_ref.dtype)


def _norm_matmul(h, g, w, out_dtype, tm, name):
    t, d = h.shape
    n = w.shape[1]
    return pl.pallas_call(
        functools.partial(_norm_matmul_kernel, n_chunk=512),
        grid=(t // tm,),
        in_specs=[pl.BlockSpec((tm, d), lambda i: (i, 0)), _resident((1, d)), _resident((d, n))],
        out_specs=pl.BlockSpec((tm, n), lambda i: (i, 0)),
        out_shape=jax.ShapeDtypeStruct((t, n), out_dtype),
        compiler_params=_params(1),
        name=name,
    )(h, g, w)


def _sb_attn_kernel(q_ref, k_ref, v_ref, qg_ref, kg_ref, o_ref,
                    qn_s, kcat_s, vcat_s, oacc_s, carry_s, *, blk):
    s_len = q_ref.shape[1]
    n_blk = s_len // blk
    scale = SB_HEAD_DIM ** -0.5

    row = lax.broadcasted_iota(jnp.int32, (LANES, LANES), 0)
    col = lax.broadcasted_iota(jnp.int32, (LANES, LANES), 1)
    head_sum = jnp.where((row >> 6) == (col >> 6), 1.0, 0.0).astype(BF16)
    head0 = lax.broadcasted_iota(jnp.int32, (blk, LANES), 1) < SB_HEAD_DIM

    def head_norm(x, gain):
        hi, lo = _split_bf16(x * x)
        ss = _dot(hi, head_sum) + _dot(lo, head_sum)
        return x * lax.rsqrt(ss * (1.0 / SB_HEAD_DIM) + EPS) * gain

    def prep(c, _):
        r = pl.ds(pl.multiple_of(c * blk, blk), blk)
        qn = head_norm(q_ref[0, r, :].astype(F32), qg_ref[...] * scale)
        qn_s[r, :] = qn.astype(BF16)
        kn = head_norm(k_ref[0, r, :].astype(F32), kg_ref[...]).astype(BF16)
        vv = v_ref[0, r, :]
        zero = jnp.zeros_like(kn)
        kcat_s[c] = jnp.concatenate([jnp.where(head0, kn, zero), jnp.where(head0, zero, kn)], axis=0)
        vcat_s[c] = jnp.concatenate([jnp.where(head0, vv, zero), jnp.where(head0, zero, vv)], axis=0)
        return 0

    lax.fori_loop(0, n_blk, prep, 0)

    r2 = lax.broadcasted_iota(jnp.int32, (2 * blk, blk), 0) & (blk - 1)
    c2 = lax.broadcasted_iota(jnp.int32, (2 * blk, blk), 1)
    cum_mat = jnp.where(r2 >= c2, 1.0, 0.0).astype(BF16)
    causal = (lax.broadcasted_iota(jnp.int32, (blk, 2 * blk), 1) & (blk - 1)) < \
        lax.broadcasted_iota(jnp.int32, (blk, 2 * blk), 0)

    def step(q_blk, kb, carry, oacc, diagonal):
        z = _dot_nt(q_blk, kcat_s[kb])
        neg_log_keep = jnp.maximum(z, 0.0) + jnp.log(1.0 + jnp.exp(-jnp.abs(z)))
        if diagonal:
            neg_log_keep = jnp.where(causal, neg_log_keep, 0.0)
        a_heads, new_carry = [], []
        for hd in range(2):
            cols = slice(hd * blk, (hd + 1) * blk)
            hi, lo = _split_bf16(neg_log_keep[:, cols])
            cs = _dot(jnp.concatenate([hi, lo], axis=1), cum_mat)
            a = jnp.exp(z[:, cols] - cs - carry[hd])
            if diagonal:
                a = jnp.where(causal[:, cols], a, 0.0)
            a_heads.append(a.astype(BF16))
            new_carry.append(carry[hd] + cs[:, 0:1])
        oacc = oacc + _dot(jnp.concatenate(a_heads, axis=1), vcat_s[kb])
        return new_carry, oacc

    def q_block(qb, first):
        r = pl.ds(pl.multiple_of(qb * blk, blk), blk)
        q_blk = qn_s[r, :]
        zeros = jnp.zeros((blk, LANES), F32)
        carry, oacc = step(q_blk, qb, [jnp.zeros((blk, 1), F32)] * 2, zeros, True)
        if first:
            o_ref[0, r, :] = oacc.astype(o_ref.dtype)
            return
        carry, oacc = step(q_blk, qb - 1, carry, oacc, False)
        carry_s[0], carry_s[1] = carry
        oacc_s[...] = oacc

        def kv_cond(st):
            kb, low = st
            return jnp.logical_and(kb >= 0, low < -LOG_ZERO_F32)

        def kv_step(st):
            kb, _ = st
            carry, oacc = step(q_blk, kb, [carry_s[0], carry_s[1]], oacc_s[...], False)
            carry_s[0], carry_s[1] = carry
            oacc_s[...] = oacc
            return kb - 1, jnp.min(jnp.minimum(carry[0], carry[1]))

        lax.while_loop(kv_cond, kv_step, (qb - 2, jnp.min(jnp.minimum(carry[0], carry[1]))))
        o_ref[0, r, :] = oacc_s[...].astype(o_ref.dtype)

    q_block(0, True)

    def q_body(qb, _):
        q_block(qb, False)
        return 0

    lax.fori_loop(1, n_blk, q_body, 0)


def _sb_attention(qkv, q_gain, k_gain, name):
    b, s, d3 = qkv.shape
    d = d3 // 3
    n_pair = d // LANES
    blk = SB_BLOCK
    n_blk = s // blk
    gains = [jnp.tile(g.astype(F32), LANES // SB_HEAD_DIM).reshape(1, LANES) for g in (q_gain, k_gain)]
    col_block = lambda off: pl.BlockSpec((1, s, LANES), lambda bi, hp: (bi, 0, off + hp))
    return pl.pallas_call(
        functools.partial(_sb_attn_kernel, blk=blk),
        grid=(b, n_pair),
        in_specs=[col_block(0), col_block(n_pair), col_block(2 * n_pair),
                  _resident((1, LANES)), _resident((1, LANES))],
        out_specs=pl.BlockSpec((1, s, LANES), lambda bi, hp: (bi, 0, hp)),
        out_shape=jax.ShapeDtypeStruct((b, s, d), BF16),
        scratch_shapes=[
            pltpu.VMEM((s, LANES), BF16),
            pltpu.VMEM((n_blk, 2 * blk, LANES), BF16),
            pltpu.VMEM((n_blk, 2 * blk, LANES), BF16),
            pltpu.VMEM((blk, LANES), F32),
            pltpu.VMEM((2, blk, 1), F32),
        ],
        compiler_params=_params(2),
        name=name,
    )(qkv, qkv, qkv, *gains)


def _pool_kernel(h_ref, g_ref, win_ref, wgrp_ref, sc_ref, o_ref, ext_s, *, tm, seq):
    i = pl.program_id(0)
    t0 = (i * tm) % seq

    @pl.when(t0 == 0)
    def _():
        ext_s[0:POOL_HALO, :] = jnp.zeros((POOL_HALO, ext_s.shape[1]), F32)

    x = h_ref[...]
    u = _dot(_rms(x, g_ref[...]).astype(BF16), win_ref[...])
    ext_s[POOL_HALO:POOL_HALO + tm, :] = u

    pos = t0 + lax.broadcasted_iota(jnp.int32, (tm, 1), 0)
    gdim = u.shape[1] // len(POOL_WINDOWS)
    for g, w in enumerate(POOL_WINDOWS):
        cols = slice(g * gdim, (g + 1) * gdim)
        ug = u[:, cols]
        win_sum = ug
        for back in range(1, w):
            win_sum = win_sum + ext_s[POOL_HALO - back:POOL_HALO - back + tm, cols]
        cnt = jnp.minimum(pos + 1, w).astype(F32)
        pooled = win_sum / cnt - ug
        y = _dot(pooled.astype(BF16), wgrp_ref[g]) * sc_ref[:, cols]
        o_ref[:, cols] = x[:, cols] + y

    ext_s[0:POOL_HALO, :] = ext_s[tm:tm + POOL_HALO, :]


def _pool_mixer(h, seq, g, w_in, w_grp, scale, tm, name):
    t, d = h.shape
    ng, gd, _ = w_grp.shape
    return pl.pallas_call(
        functools.partial(_pool_kernel, tm=tm, seq=seq),
        grid=(t // tm,),
        in_specs=[pl.BlockSpec((tm, d), lambda i: (i, 0)), _resident((1, d)), _resident((d, d)),
                  _resident((ng, gd, gd)), _resident((1, d))],
        out_specs=pl.BlockSpec((tm, d), lambda i: (i, 0)),
        out_shape=jax.ShapeDtypeStruct((t, d), F32),
        scratch_shapes=[pltpu.VMEM((POOL_HALO + tm, d), F32)],
        compiler_params=_params(1),
        name=name,
    )(h, g, w_in, w_grp, scale)


_HGRN_LEVELS = (64, 32, 16, 8, 4, 2, 1)


def _hgrn_kernel(q_ref, f_ref, i_ref, g_ref, lbp_ref, og_ref, o_ref, cum_s, state_s, *, layer_idx):
    c = HGRN_CHUNK
    n_chunk = q_ref.shape[1] // c

    lbp = lbp_ref[...]
    e = jnp.exp(lbp - jnp.max(lbp, axis=0, keepdims=True))
    sm = e / jnp.sum(e, axis=0, keepdims=True)
    lb = jnp.sum(sm[1:layer_idx + 1], axis=0, keepdims=True)
    log_lb = jnp.log(lb)
    log_1m_lb = jnp.log1p(-lb)

    r = lax.broadcasted_iota(jnp.int32, (c, c), 0)
    s = lax.broadcasted_iota(jnp.int32, (c, c), 1)
    one = lambda m: jnp.where(m, 1.0, 0.0).astype(BF16)
    cum_s[0:c, :] = one(s <= r)
    cum_s[c:2 * c, :] = one(s > r)
    for l, m in enumerate(_HGRN_LEVELS):
        sh = m.bit_length() - 1
        same = (r >> sh) == (s >> sh)
        odd = ((r >> sh) & 1) == 1
        cum_s[(2 + l) * c:(3 + l) * c, :] = one(same & (odd == (s <= r)))

    state_s[...] = jnp.zeros_like(state_s)

    def chunk(n, _):
        rows = pl.ds(pl.multiple_of(n * c, c), c)
        fg = f_ref[0, rows, :].astype(F32)
        a_ = log_lb
        c_ = log_1m_lb + _log_sigmoid(fg)
        log_f = jnp.maximum(a_, c_) + jnp.log(1.0 + jnp.exp(-jnp.abs(a_ - c_)))
        k = (1.0 - lb) * jax.nn.sigmoid(-fg)
        q = _silu(q_ref[0, rows, :].astype(F32))
        v = i_ref[0, rows, :]

        hi, lo = _split_bf16(log_f)
        cum = _dot(cum_s[...], hi) + _dot(cum_s[...], lo)
        dec = jnp.exp(cum)
        b_end = dec[c - 1:c, :]

        state = state_s[...]
        o = _dot_nt((q * dec[0:c]).astype(BF16), state.astype(BF16))
        kd = (k * dec[c:2 * c]).astype(BF16)
        v_t = v.astype(F32).T.astype(BF16)
        state_s[...] = state * b_end + _dot(v_t, kd)

        scores = jnp.zeros((c, c), F32)
        for l, m in enumerate(_HGRN_LEVELS):
            sh = m.bit_length() - 1
            d_l = dec[(2 + l) * c:(3 + l) * c]
            sc = _dot_nt((q * d_l).astype(BF16), (k * d_l).astype(BF16))
            pair = (((r >> sh) & 1) == 1) & ((s >> sh) == (r >> sh) - 1)
            scores = scores + jnp.where(pair, sc, 0.0)
        o = o + _dot(scores.astype(BF16), v)
        o = o + jnp.sum(q * k, axis=-1, keepdims=True) * v.astype(F32)

        o = _rms(o, og_ref[...]) * _silu(g_ref[0, rows, :].astype(F32))
        o_ref[0, rows, :] = o.astype(o_ref.dtype)
        return 0

    lax.fori_loop(0, n_chunk, chunk, 0, unroll=2)


def _hgrn_core(proj, lower_bounds, layer_idx, o_gain, name):
    b, s, d4 = proj.shape
    d = d4 // 4
    n_head = d // LANES
    depth = lower_bounds.shape[0]
    col_block = lambda off: pl.BlockSpec((1, s, LANES), lambda bi, hd: (bi, 0, off + hd))
    return pl.pallas_call(
        functools.partial(_hgrn_kernel, layer_idx=layer_idx),
        grid=(b, n_head),
        in_specs=[col_block(0), col_block(n_head), col_block(2 * n_head), col_block(3 * n_head),
                  pl.BlockSpec((depth, LANES), lambda bi, hd: (0, hd)), _resident((1, LANES))],
        out_specs=pl.BlockSpec((1, s, LANES), lambda bi, hd: (bi, 0, hd)),
        out_shape=jax.ShapeDtypeStruct((b, s, d), BF16),
        scratch_shapes=[pltpu.VMEM(((2 + len(_HGRN_LEVELS)) * HGRN_CHUNK, HGRN_CHUNK), BF16),
                        pltpu.VMEM((LANES, LANES), F32)],
        compiler_params=_params(2),
        name=name,
    )(proj, proj, proj, proj, lower_bounds, o_gain.reshape(1, LANES))


def _swiglu_into(acc_ref, hn, wgu, wd, d_ff, ck, row_scale=None):
    for c0 in range(0, d_ff, ck):
        c1 = min(c0 + ck, d_ff)
        gate = _dot(hn, wgu[:, c0:c1])
        up = _dot(hn, wgu[:, d_ff + c0:d_ff + c1])
        act = _silu(gate) * up
        if row_scale is not None:
            act = act * row_scale
        acc_ref[...] += _dot(act.astype(BF16), wd[c0:c1, :])


def _ple(h, p, gn, wg, wp):
    gate = jax.nn.sigmoid(_dot(_rms(h, gn).astype(BF16), wg))
    return h + gate * _dot(p.astype(BF16), wp)


def _dense_post_kernel(h_ref, y_ref, p_ref, wo_ref, nf_ref, wgu_ref, wd_ref, gn_ref, wg_ref, wp_ref,
                       o_ref, acc_ref, *, ck):
    h1 = h_ref[...] + _dot(y_ref[...], wo_ref[...])
    acc_ref[...] = h1
    hn = _rms(h1, nf_ref[...]).astype(BF16)
    _swiglu_into(acc_ref, hn, wgu_ref, wd_ref, wd_ref.shape[0], ck)
    o_ref[...] = _ple(acc_ref[...], p_ref[...], gn_ref[...], wg_ref[...], wp_ref[...])


def _dense_post(h, y, p, wo, nf, wgu, wd, gn, wg, wp, tm, name):
    t, d = h.shape
    tile = lambda n: pl.BlockSpec((tm, n), lambda i: (i, 0))
    weights = (wo, nf, wgu, wd, gn, wg, wp)
    return pl.pallas_call(
        functools.partial(_dense_post_kernel, ck=256),
        grid=(t // tm,),
        in_specs=[tile(d), tile(d), tile(p.shape[1])] + [_resident(w.shape) for w in weights],
        out_specs=tile(d),
        out_shape=jax.ShapeDtypeStruct((t, d), F32),
        scratch_shapes=[pltpu.VMEM((tm, d), F32)],
        compiler_params=_params(1),
        name=name,
    )(h, y, p, *weights)


def _router(hn, wr_ref, n_exp):
    logits = _dot(hn, wr_ref[...])
    idx = lax.broadcasted_iota(jnp.int32, logits.shape, 1)
    m1 = jnp.max(logits, axis=-1, keepdims=True)
    i1 = jnp.min(jnp.where(logits == m1, idx, n_exp), axis=-1, keepdims=True)
    rest = jnp.where(idx == i1, -jnp.inf, logits)
    m2 = jnp.max(rest, axis=-1, keepdims=True)
    i2 = jnp.min(jnp.where(rest == m2, idx, n_exp), axis=-1, keepdims=True)
    e2 = jnp.exp(m2 - m1)
    g1 = 1.0 / (1.0 + e2)
    return jnp.where(idx == i1, g1, 0.0) + jnp.where(idx == i2, e2 * g1, 0.0)


def _route_proj_kernel(h_ref, y_ref, wo_ref, nf_ref, wr_ref, h1_ref, hn_ref, cmb_ref):
    h1 = h_ref[...] + _dot(y_ref[...], wo_ref[...])
    h1_ref[...] = h1
    hn = _rms(h1, nf_ref[...]).astype(BF16)
    hn_ref[...] = hn
    cmb_ref[...] = _router(hn, wr_ref, cmb_ref.shape[1])


def _route_kernel(h_ref, nf_ref, wr_ref, hn_ref, cmb_ref):
    hn = _rms(h_ref[...], nf_ref[...]).astype(BF16)
    hn_ref[...] = hn
    cmb_ref[...] = _router(hn, wr_ref, cmb_ref.shape[1])


def _route(h, y, wo, nf, wr, tm, name):
    t, d = h.shape
    n_exp = wr.shape[1]
    tile = lambda n: pl.BlockSpec((tm, n), lambda i: (i, 0))
    hn_cmb = [jax.ShapeDtypeStruct((t, d), BF16), jax.ShapeDtypeStruct((t, n_exp), F32)]
    if y is None:
        hn, cmb = pl.pallas_call(
            _route_kernel, grid=(t // tm,),
            in_specs=[tile(d), _resident(nf.shape), _resident(wr.shape)],
            out_specs=[tile(d), tile(n_exp)], out_shape=hn_cmb,
            compiler_params=_params(1), name=name,
        )(h, nf, wr)
        return h, hn, cmb
    return pl.pallas_call(
        _route_proj_kernel, grid=(t // tm,),
        in_specs=[tile(d), tile(d), _resident(wo.shape), _resident(nf.shape), _resident(wr.shape)],
        out_specs=[tile(d), tile(d), tile(n_exp)],
        out_shape=[jax.ShapeDtypeStruct((t, d), F32)] + hn_cmb,
        compiler_params=_params(1), name=name,
    )(h, y, wo, nf, wr)


def _moe_kernel(h1_ref, hn_ref, cmb_ref, p_ref, wgu_ref, wd_ref, gn_ref, wg_ref, wp_ref,
                o_ref, acc_ref, *, ck):
    e = pl.program_id(1)

    @pl.when(e == 0)
    def _():
        acc_ref[...] = h1_ref[...]

    cmb = cmb_ref[...]
    lane = lax.broadcasted_iota(jnp.int32, cmb.shape, 1)
    w_e = jnp.sum(jnp.where(lane == e, cmb, 0.0), axis=-1, keepdims=True)
    _swiglu_into(acc_ref, hn_ref[...], wgu_ref.at[0], wd_ref.at[0], wd_ref.shape[1], ck, row_scale=w_e)

    @pl.when(e == pl.num_programs(1) - 1)
    def _():
        o_ref[...] = _ple(acc_ref[...], p_ref[...], gn_ref[...], wg_ref[...], wp_ref[...])


def _moe_post(h1, hn, cmb, p, wgu, wd, gn, wg, wp, tm, name):
    t, d = h1.shape
    n_exp, d_ffe, _ = wd.shape
    tile = lambda n: pl.BlockSpec((tm, n), lambda i, e: (i, 0))
    return pl.pallas_call(
        functools.partial(_moe_kernel, ck=256),
        grid=(t // tm, n_exp),
        in_specs=[tile(d), tile(d), tile(n_exp), tile(p.shape[1]),
                  pl.BlockSpec((1, d, 2 * d_ffe), lambda i, e: (e, 0, 0)),
                  pl.BlockSpec((1, d_ffe, d), lambda i, e: (e, 0, 0)),
                  _resident(gn.shape), _resident(wg.shape), _resident(wp.shape)],
        out_specs=tile(d),
        out_shape=jax.ShapeDtypeStruct((t, d), F32),
        scratch_shapes=[pltpu.VMEM((tm, d), F32)],
        compiler_params=_params(2),
        name=name,
    )(h1, hn, cmb, p, wgu, wd, gn, wg, wp)


def kernel(x, p, norm_mix, norm_ffn, sb_w_qkv, sb_q_gain, sb_k_gain, sb_w_o, pool_w_in, pool_w_grp,
           pool_scale, hgrn_w_in, hgrn_lower_bounds, hgrn_o_gain, hgrn_w_o, ffn_w_gu, ffn_w_d,
           moe_router, moe_w_gu, moe_w_d, ple_w, ple_gate_norm, ple_gate_w):
    b, s, d = x.shape
    depth = p.shape[0]
    t = b * s
    tm = 512
    bf = lambda w: w.astype(BF16)
    vec = lambda g: g.reshape(1, -1)

    h = x.reshape(t, d)
    for i in range(depth):
        kind, j = i % N_MIXERS, i // N_MIXERS
        p_i = p[i].reshape(t, -1)
        ple = (vec(ple_gate_norm[i]), bf(ple_gate_w[i]), bf(ple_w[i]))
        y = w_o = None
        if kind == 0:
            qkv = _norm_matmul(h, vec(norm_mix[i]), bf(sb_w_qkv[j]), BF16, tm, f"sb_qkv_{i}")
            y = _sb_attention(qkv.reshape(b, s, 3 * d), sb_q_gain[j], sb_k_gain[j], f"sb_attn_{i}")
            y, w_o = y.reshape(t, d), bf(sb_w_o[j])
        elif kind == 1:
            h = _pool_mixer(h, s, vec(norm_mix[i]), bf(pool_w_in[j]), bf(pool_w_grp[j]),
                            vec(pool_scale[j]), tm, f"pool_{i}")
        else:
            proj = _norm_matmul(h, vec(norm_mix[i]), bf(hgrn_w_in[j]), BF16, tm, f"hgrn_in_{i}")
            y = _hgrn_core(proj.reshape(b, s, 4 * d), hgrn_lower_bounds, i, hgrn_o_gain[j], f"hgrn_{i}")
            y, w_o = y.reshape(t, d), bf(hgrn_w_o[j])

        c = i // 2
        if i % 2 == 0:
            h = _dense_post(h, y, p_i, w_o, vec(norm_ffn[i]), bf(ffn_w_gu[c]), bf(ffn_w_d[c]), *ple,
                            tm, f"dense_{i}")
        else:
            h1, hn, cmb = _route(h, y, w_o, vec(norm_ffn[i]), bf(moe_router[c]), tm, f"route_{i}")
            h = _moe_post(h1, hn, cmb, p_i, bf(moe_w_gu[c]), bf(moe_w_d[c]), *ple, 2 * tm, f"moe_{i}")
    return h.reshape(b, s, d)
```

```python
import functools

import jax
import jax.numpy as jnp
from jax import lax
from jax.experimental import pallas as pl
from jax.experimental.pallas import tpu as pltpu
from jax.experimental.pallas import tpu_sc as plsc

F32 = jnp.float32
BF16 = jnp.bfloat16

EPS = 1e-6
N_MIXERS = 3
SB_HEAD_DIM = 64
POOL_WINDOWS = (2, 4, 8, 16)
HGRN_CHUNK = 128
TOP_K = 2

LANES = 128
SB_BLOCK = 256
POOL_HALO = 16
EXPERT_ROW_TILE = 512
SC_CORES = 2
SC_WORKERS = SC_CORES * 16
SC_CHUNK = 64
VMEM_LIMIT = 56 * 1024 * 1024
LOG_ZERO_F32 = -104.0


def _dot(a, b):
    return jnp.dot(a, b, preferred_element_type=F32)


def _dot_nt(a, b):
    return lax.dot_general(a, b, (((1,), (1,)), ((), ())), preferred_element_type=F32)


def _split_bf16(x):
    hi = x.astype(BF16)
    lo = (x - hi.astype(F32)).astype(BF16)
    return hi, lo


def _rms(x, g):
    ms = jnp.mean(x * x, axis=-1, keepdims=True)
    return x * lax.rsqrt(ms + EPS) * g


def _log_sigmoid(x):
    return -(jnp.maximum(-x, 0.0) + jnp.log(1.0 + jnp.exp(-jnp.abs(x))))


def _silu(x):
    return x * jax.nn.sigmoid(x)


def _params(n_axes):
    return pltpu.CompilerParams(
        dimension_semantics=("arbitrary",) * n_axes, vmem_limit_bytes=VMEM_LIMIT)


def _resident(shape):
    nd = len(shape)
    return pl.BlockSpec(shape, lambda *_: (0,) * nd)


def _norm_matmul_kernel(h_ref, g_ref, w_ref, o_ref, *, n_chunk):
    xn = _rms(h_ref[...], g_ref[...]).astype(BF16)
    for c0 in range(0, w_ref.shape[1], n_chunk):
        o_ref[:, c0:c0 + n_chunk] = _dot(xn, w_ref[:, c0:c0 + n_chunk]).astype(o_ref.dtype)


def _norm_matmul(h, g, w, out_dtype, tm, name):
    t, d = h.shape
    n = w.shape[1]
    return pl.pallas_call(
        functools.partial(_norm_matmul_kernel, n_chunk=512),
        grid=(t // tm,),
        in_specs=[pl.BlockSpec((tm, d), lambda i: (i, 0)), _resident((1, d)), _resident((d, n))],
        out_specs=pl.BlockSpec((tm, n), lambda i: (i, 0)),
        out_shape=jax.ShapeDtypeStruct((t, n), out_dtype),
        compiler_params=_params(1),
        name=name,
    )(h, g, w)


def _sb_attn_kernel(q_ref, k_ref, v_ref, qg_ref, kg_ref, o_ref,
                    qn_s, kcat_s, vcat_s, oacc_s, carry_s, *, blk):
    s_len = q_ref.shape[1]
    n_blk = s_len // blk
    scale = SB_HEAD_DIM ** -0.5

    row = lax.broadcasted_iota(jnp.int32, (LANES, LANES), 0)
    col = lax.broadcasted_iota(jnp.int32, (LANES, LANES), 1)
    head_sum = jnp.where((row >> 6) == (col >> 6), 1.0, 0.0).astype(BF16)
    head0 = lax.broadcasted_iota(jnp.int32, (blk, LANES), 1) < SB_HEAD_DIM

    def head_norm(x, gain):
        hi, lo = _split_bf16(x * x)
        ss = _dot(hi, head_sum) + _dot(lo, head_sum)
        return x * lax.rsqrt(ss * (1.0 / SB_HEAD_DIM) + EPS) * gain

    def prep(c, _):
        r = pl.ds(pl.multiple_of(c * blk, blk), blk)
        qn = head_norm(q_ref[0, r, :].astype(F32), qg_ref[...] * scale)
        qn_s[r, :] = qn.astype(BF16)
        kn = head_norm(k_ref[0, r, :].astype(F32), kg_ref[...]).astype(BF16)
        vv = v_ref[0, r, :]
        zero = jnp.zeros_like(kn)
        kcat_s[c] = jnp.concatenate([jnp.where(head0, kn, zero), jnp.where(head0, zero, kn)], axis=0)
        vcat_s[c] = jnp.concatenate([jnp.where(head0, vv, zero), jnp.where(head0, zero, vv)], axis=0)
        return 0

    lax.fori_loop(0, n_blk, prep, 0)

    r2 = lax.broadcasted_iota(jnp.int32, (2 * blk, blk), 0) & (blk - 1)
    c2 = lax.broadcasted_iota(jnp.int32, (2 * blk, blk), 1)
    cum_mat = jnp.where(r2 >= c2, 1.0, 0.0).astype(BF16)
    causal = (lax.broadcasted_iota(jnp.int32, (blk, 2 * blk), 1) & (blk - 1)) < \
        lax.broadcasted_iota(jnp.int32, (blk, 2 * blk), 0)

    def step(q_blk, kb, carry, oacc, diagonal):
        z = _dot_nt(q_blk, kcat_s[kb])
        neg_log_keep = jnp.maximum(z, 0.0) + jnp.log(1.0 + jnp.exp(-jnp.abs(z)))
        if diagonal:
            neg_log_keep = jnp.where(causal, neg_log_keep, 0.0)
        a_heads, new_carry = [], []
        for hd in range(2):
            cols = slice(hd * blk, (hd + 1) * blk)
            hi, lo = _split_bf16(neg_log_keep[:, cols])
            cs = _dot(jnp.concatenate([hi, lo], axis=1), cum_mat)
            a = jnp.exp(z[:, cols] - cs - carry[hd])
            if diagonal:
                a = jnp.where(causal[:, cols], a, 0.0)
            a_heads.append(a.astype(BF16))
            new_carry.append(carry[hd] + cs[:, 0:1])
        oacc = oacc + _dot(jnp.concatenate(a_heads, axis=1), vcat_s[kb])
        return new_carry, oacc

    def q_block(qb, first):
        r = pl.ds(pl.multiple_of(qb * blk, blk), blk)
        q_blk = qn_s[r, :]
        zeros = jnp.zeros((blk, LANES), F32)
        carry, oacc = step(q_blk, qb, [jnp.zeros((blk, 1), F32)] * 2, zeros, True)
        if first:
            o_ref[0, r, :] = oacc.astype(o_ref.dtype)
            return
        carry, oacc = step(q_blk, qb - 1, carry, oacc, False)
        carry_s[0], carry_s[1] = carry
        oacc_s[...] = oacc

        def kv_cond(st):
            kb, low = st
            return jnp.logical_and(kb >= 0, low < -LOG_ZERO_F32)

        def kv_step(st):
            kb, _ = st
            carry, oacc = step(q_blk, kb, [carry_s[0], carry_s[1]], oacc_s[...], False)
            carry_s[0], carry_s[1] = carry
            oacc_s[...] = oacc
            return kb - 1, jnp.min(jnp.minimum(carry[0], carry[1]))

        lax.while_loop(kv_cond, kv_step, (qb - 2, jnp.min(jnp.minimum(carry[0], carry[1]))))
        o_ref[0, r, :] = oacc_s[...].astype(o_ref.dtype)

    q_block(0, True)

    def q_body(qb, _):
        q_block(qb, False)
        return 0

    lax.fori_loop(1, n_blk, q_body, 0)


def _sb_attention(qkv, q_gain, k_gain, name):
    b, s, d3 = qkv.shape
    d = d3 // 3
    n_pair = d // LANES
    blk = SB_BLOCK
    n_blk = s // blk
    gains = [jnp.tile(g.astype(F32), LANES // SB_HEAD_DIM).reshape(1, LANES) for g in (q_gain, k_gain)]
    col_block = lambda off: pl.BlockSpec((1, s, LANES), lambda bi, hp: (bi, 0, off + hp))
    return pl.pallas_call(
        functools.partial(_sb_attn_kernel, blk=blk),
        grid=(b, n_pair),
        in_specs=[col_block(0), col_block(n_pair), col_block(2 * n_pair),
                  _resident((1, LANES)), _resident((1, LANES))],
        out_specs=pl.BlockSpec((1, s, LANES), lambda bi, hp: (bi, 0, hp)),
        out_shape=jax.ShapeDtypeStruct((b, s, d), BF16),
        scratch_shapes=[
            pltpu.VMEM((s, LANES), BF16),
            pltpu.VMEM((n_blk, 2 * blk, LANES), BF16),
            pltpu.VMEM((n_blk, 2 * blk, LANES), BF16),
            pltpu.VMEM((blk, LANES), F32),
            pltpu.VMEM((2, blk, 1), F32),
        ],
        compiler_params=_params(2),
        name=name,
    )(qkv, qkv, qkv, *gains)


def _pool_kernel(h_ref, g_ref, win_ref, wgrp_ref, sc_ref, o_ref, ext_s, *, tm, seq):
    i = pl.program_id(0)
    t0 = (i * tm) % seq

    @pl.when(t0 == 0)
    def _():
        ext_s[0:POOL_HALO, :] = jnp.zeros((POOL_HALO, ext_s.shape[1]), F32)

    x = h_ref[...]
    u = _dot(_rms(x, g_ref[...]).astype(BF16), win_ref[...])
    ext_s[POOL_HALO:POOL_HALO + tm, :] = u

    pos = t0 + lax.broadcasted_iota(jnp.int32, (tm, 1), 0)
    gdim = u.shape[1] // len(POOL_WINDOWS)
    for g, w in enumerate(POOL_WINDOWS):
        cols = slice(g * gdim, (g + 1) * gdim)
        ug = u[:, cols]
        win_sum = ug
        for back in range(1, w):
            win_sum = win_sum + ext_s[POOL_HALO - back:POOL_HALO - back + tm, cols]
        cnt = jnp.minimum(pos + 1, w).astype(F32)
        pooled = win_sum / cnt - ug
        y = _dot(pooled.astype(BF16), wgrp_ref[g]) * sc_ref[:, cols]
        o_ref[:, cols] = x[:, cols] + y

    ext_s[0:POOL_HALO, :] = ext_s[tm:tm + POOL_HALO, :]


def _pool_mixer(h, seq, g, w_in, w_grp, scale, tm, name):
    t, d = h.shape
    ng, gd, _ = w_grp.shape
    return pl.pallas_call(
        functools.partial(_pool_kernel, tm=tm, seq=seq),
        grid=(t // tm,),
        in_specs=[pl.BlockSpec((tm, d), lambda i: (i, 0)), _resident((1, d)), _resident((d, d)),
                  _resident((ng, gd, gd)), _resident((1, d))],
        out_specs=pl.BlockSpec((tm, d), lambda i: (i, 0)),
        out_shape=jax.ShapeDtypeStruct((t, d), F32),
        scratch_shapes=[pltpu.VMEM((POOL_HALO + tm, d), F32)],
        compiler_params=_params(1),
        name=name,
    )(h, g, w_in, w_grp, scale)


_HGRN_LEVELS = (64, 32, 16, 8, 4, 2, 1)


def _hgrn_kernel(q_ref, f_ref, i_ref, g_ref, lbp_ref, og_ref, o_ref, cum_s, state_s, *, layer_idx):
    c = HGRN_CHUNK
    n_chunk = q_ref.shape[1] // c

    lbp = lbp_ref[...]
    e = jnp.exp(lbp - jnp.max(lbp, axis=0, keepdims=True))
    sm = e / jnp.sum(e, axis=0, keepdims=True)
    lb = jnp.sum(sm[1:layer_idx + 1], axis=0, keepdims=True)
    log_lb = jnp.log(lb)
    log_1m_lb = jnp.log1p(-lb)

    r = lax.broadcasted_iota(jnp.int32, (c, c), 0)
    s = lax.broadcasted_iota(jnp.int32, (c, c), 1)
    one = lambda m: jnp.where(m, 1.0, 0.0).astype(BF16)
    cum_s[0:c, :] = one(s <= r)
    cum_s[c:2 * c, :] = one(s > r)
    for l, m in enumerate(_HGRN_LEVELS):
        sh = m.bit_length() - 1
        same = (r >> sh) == (s >> sh)
        odd = ((r >> sh) & 1) == 1
        cum_s[(2 + l) * c:(3 + l) * c, :] = one(same & (odd == (s <= r)))

    state_s[...] = jnp.zeros_like(state_s)

    def chunk(n, _):
        rows = pl.ds(pl.multiple_of(n * c, c), c)
        fg = f_ref[0, rows, :].astype(F32)
        a_ = log_lb
        c_ = log_1m_lb + _log_sigmoid(fg)
        log_f = jnp.maximum(a_, c_) + jnp.log(1.0 + jnp.exp(-jnp.abs(a_ - c_)))
        k = (1.0 - lb) * jax.nn.sigmoid(-fg)
        q = _silu(q_ref[0, rows, :].astype(F32))
        v = i_ref[0, rows, :]

        hi, lo = _split_bf16(log_f)
        cum = _dot(cum_s[...], hi) + _dot(cum_s[...], lo)
        dec = jnp.exp(cum)
        b_end = dec[c - 1:c, :]

        state = state_s[...]
        o = _dot_nt((q * dec[0:c]).astype(BF16), state.astype(BF16))
        kd = (k * dec[c:2 * c]).astype(BF16)
        v_t = v.astype(F32).T.astype(BF16)
        state_s[...] = state * b_end + _dot(v_t, kd)

        scores = jnp.zeros((c, c), F32)
        for l, m in enumerate(_HGRN_LEVELS):
            sh = m.bit_length() - 1
            d_l = dec[(2 + l) * c:(3 + l) * c]
            sc = _dot_nt((q * d_l).astype(BF16), (k * d_l).astype(BF16))
            pair = (((r >> sh) & 1) == 1) & ((s >> sh) == (r >> sh) - 1)
            scores = scores + jnp.where(pair, sc, 0.0)
        o = o + _dot(scores.astype(BF16), v)
        o = o + jnp.sum(q * k, axis=-1, keepdims=True) * v.astype(F32)

        o = _rms(o, og_ref[...]) * _silu(g_ref[0, rows, :].astype(F32))
        o_ref[0, rows, :] = o.astype(o_ref.dtype)
        return 0

    lax.fori_loop(0, n_chunk, chunk, 0, unroll=2)


def _hgrn_core(proj, lower_bounds, layer_idx, o_gain, name):
    b, s, d4 = proj.shape
    d = d4 // 4
    n_head = d // LANES
    depth = lower_bounds.shape[0]
    col_block = lambda off: pl.BlockSpec((1, s, LANES), lambda bi, hd: (bi, 0, off + hd))
    return pl.pallas_call(
        functools.partial(_hgrn_kernel, layer_idx=layer_idx),
        grid=(b, n_head),
        in_specs=[col_block(0), col_block(n_head), col_block(2 * n_head), col_block(3 * n_head),
                  pl.BlockSpec((depth, LANES), lambda bi, hd: (0, hd)), _resident((1, LANES))],
        out_specs=pl.BlockSpec((1, s, LANES), lambda bi, hd: (bi, 0, hd)),
        out_shape=jax.ShapeDtypeStruct((b, s, d), BF16),
        scratch_shapes=[pltpu.VMEM(((2 + len(_HGRN_LEVELS)) * HGRN_CHUNK, HGRN_CHUNK), BF16),
                        pltpu.VMEM((LANES, LANES), F32)],
        compiler_params=_params(2),
        name=name,
    )(proj, proj, proj, proj, lower_bounds, o_gain.reshape(1, LANES))


def _swiglu_into(acc_ref, hn, wgu, wd, d_ff, ck, row_scale=None):
    for c0 in range(0, d_ff, ck):
        c1 = min(c0 + ck, d_ff)
        gate = _dot(hn, wgu[:, c0:c1])
        up = _dot(hn, wgu[:, d_ff + c0:d_ff + c1])
        act = _silu(gate) * up
        if row_scale is not None:
            act = act * row_scale
        acc_ref[...] += _dot(act.astype(BF16), wd[c0:c1, :])


def _ple(h, p, gn, wg, wp):
    gate = jax.nn.sigmoid(_dot(_rms(h, gn).astype(BF16), wg))
    return h + gate * _dot(p.astype(BF16), wp)


def _dense_post_kernel(h_ref, y_ref, p_ref, wo_ref, nf_ref, wgu_ref, wd_ref, gn_ref, wg_ref, wp_ref,
                       o_ref, acc_ref, *, ck):
    h1 = h_ref[...] + _dot(y_ref[...], wo_ref[...])
    acc_ref[...] = h1
    hn = _rms(h1, nf_ref[...]).astype(BF16)
    _swiglu_into(acc_ref, hn, wgu_ref, wd_ref, wd_ref.shape[0], ck)
    o_ref[...] = _ple(acc_ref[...], p_ref[...], gn_ref[...], wg_ref[...], wp_ref[...])


def _dense_post(h, y, p, wo, nf, wgu, wd, gn, wg, wp, tm, name):
    t, d = h.shape
    tile = lambda n: pl.BlockSpec((tm, n), lambda i: (i, 0))
    weights = (wo, nf, wgu, wd, gn, wg, wp)
    return pl.pallas_call(
        functools.partial(_dense_post_kernel, ck=256),
        grid=(t // tm,),
        in_specs=[tile(d), tile(d), tile(p.shape[1])] + [_resident(w.shape) for w in weights],
        out_specs=tile(d),
        out_shape=jax.ShapeDtypeStruct((t, d), F32),
        scratch_shapes=[pltpu.VMEM((tm, d), F32)],
        compiler_params=_params(1),
        name=name,
    )(h, y, p, *weights)


def _pack_halves(x):
    n = x.shape[1] // 2
    bits = lambda v: lax.bitcast_convert_type(v.astype(BF16).astype(F32), jnp.int32)
    return bits(x[:, n:]) | lax.shift_right_logical(bits(x[:, :n]), 16)


def _unpack_halves(u):
    lo = lax.bitcast_convert_type(lax.shift_left(u, 16), F32)
    hi = lax.bitcast_convert_type(u & jnp.int32(-65536), F32)
    return jnp.concatenate([lo.astype(BF16), hi.astype(BF16)], axis=1)


def _route_tail(h1, nf_ref, wr_ref, cnt_s, hnp_ref, sel_ref, gate_ref, cnt_ref):
    @pl.when(pl.program_id(0) == 0)
    def _():
        cnt_s[...] = jnp.zeros_like(cnt_s)

    hn = _rms(h1, nf_ref[...]).astype(BF16)
    hnp_ref[...] = _pack_halves(hn)
    logits = _dot(hn, wr_ref[...])
    tm, n_exp = logits.shape
    idx = lax.broadcasted_iota(jnp.int32, logits.shape, 1)
    m1 = jnp.max(logits, axis=-1, keepdims=True)
    i1 = jnp.min(jnp.where(logits == m1, idx, n_exp), axis=-1, keepdims=True)
    rest = jnp.where(idx == i1, -jnp.inf, logits)
    m2 = jnp.max(rest, axis=-1, keepdims=True)
    i2 = jnp.min(jnp.where(rest == m2, idx, n_exp), axis=-1, keepdims=True)
    e2 = jnp.exp(m2 - m1)
    g1 = 1.0 / (1.0 + e2)

    onehot = jnp.where(idx == i1, 1.0, jnp.where(idx == i2, 1.0, 0.0))
    r = lax.broadcasted_iota(jnp.int32, (tm, tm), 0)
    c = lax.broadcasted_iota(jnp.int32, (tm, tm), 1)
    before = cnt_s[...] + _dot(jnp.where(c < r, 1.0, 0.0).astype(BF16), onehot.astype(BF16))
    rank1 = jnp.sum(jnp.where(idx == i1, before, 0.0), axis=-1, keepdims=True).astype(jnp.int32)
    rank2 = jnp.sum(jnp.where(idx == i2, before, 0.0), axis=-1, keepdims=True).astype(jnp.int32)
    cnt_s[...] += jnp.sum(onehot, axis=0, keepdims=True)
    cnt_ref[...] = cnt_s[...]

    lane = lax.broadcasted_iota(jnp.int32, sel_ref.shape, 1)
    sel_ref[...] = jnp.where(lane == 0, i1, jnp.where(lane == 1, i2, jnp.where(lane == 2, rank1, rank2)))
    gate_ref[...] = jnp.where(lax.broadcasted_iota(jnp.int32, gate_ref.shape, 1) == 0, g1, e2 * g1)


def _route_proj_kernel(h_ref, y_ref, wo_ref, nf_ref, wr_ref, h1_ref, hnp_ref, sel_ref, gate_ref, cnt_ref,
                       cnt_s):
    h1 = h_ref[...] + _dot(y_ref[...], wo_ref[...])
    h1_ref[...] = h1
    _route_tail(h1, nf_ref, wr_ref, cnt_s, hnp_ref, sel_ref, gate_ref, cnt_ref)


def _route_kernel(h_ref, nf_ref, wr_ref, hnp_ref, sel_ref, gate_ref, cnt_ref, cnt_s):
    _route_tail(h_ref[...], nf_ref, wr_ref, cnt_s, hnp_ref, sel_ref, gate_ref, cnt_ref)


def _route(h, y, wo, nf, wr, tm, name):
    t, d = h.shape
    n_exp = wr.shape[1]
    tile = lambda n: pl.BlockSpec((tm, n), lambda i: (i, 0))
    outs = [jax.ShapeDtypeStruct((t, d // 2), jnp.int32), jax.ShapeDtypeStruct((t, 4), jnp.int32),
            jax.ShapeDtypeStruct((t, TOP_K), F32), jax.ShapeDtypeStruct((1, n_exp), F32)]
    out_specs = [tile(d // 2), tile(4), tile(TOP_K), _resident((1, n_exp))]
    scratch = [pltpu.VMEM((1, n_exp), F32)]
    if y is None:
        res = pl.pallas_call(
            _route_kernel, grid=(t // tm,),
            in_specs=[tile(d), _resident(nf.shape), _resident(wr.shape)],
            out_specs=out_specs, out_shape=outs, scratch_shapes=scratch,
            compiler_params=_params(1), name=name,
        )(h, nf, wr)
        return (h, *res)
    return pl.pallas_call(
        _route_proj_kernel, grid=(t // tm,),
        in_specs=[tile(d), tile(d), _resident(wo.shape), _resident(nf.shape), _resident(wr.shape)],
        out_specs=[tile(d)] + out_specs,
        out_shape=[jax.ShapeDtypeStruct((t, d), F32)] + outs, scratch_shapes=scratch,
        compiler_params=_params(1), name=name,
    )(h, y, wo, nf, wr)


def _sc_mesh():
    return plsc.VectorSubcoreMesh(core_axis_name="c", subcore_axis_name="s")


def _sc_worker():
    return lax.axis_index("s") * SC_CORES + lax.axis_index("c")


def _sc_scatter_rows(x, idx, n_out):
    r, d = x.shape
    per_w = idx.shape[0] // SC_WORKERS
    n_chunk = per_w // SC_CHUNK

    @functools.partial(
        pl.kernel, mesh=_sc_mesh(), out_type=jax.ShapeDtypeStruct((n_out, d), x.dtype),
        scratch_types=[pltpu.VMEM((SC_CHUNK,), jnp.int32), pltpu.VMEM((SC_CHUNK, d), x.dtype),
                       pltpu.SemaphoreType.DMA])
    def scatter(x_hbm, idx_hbm, out_hbm, idx_v, rows_v, sem):
        base = _sc_worker() * per_w

        @pl.loop(0, n_chunk)
        def _(j):
            off = pl.multiple_of(base + j * SC_CHUNK, SC_CHUNK)
            src = pl.multiple_of(off % r, SC_CHUNK)
            pltpu.sync_copy(idx_hbm.at[pl.ds(off, SC_CHUNK)], idx_v)
            pltpu.sync_copy(x_hbm.at[pl.ds(src, SC_CHUNK)], rows_v)
            pltpu.async_copy(rows_v, out_hbm.at[idx_v], sem).wait()

    return scatter(x, idx)


def _sc_gather_rows(table, idx):
    d = table.shape[1]
    b = idx.shape[0]
    per_w = b // SC_WORKERS
    n_chunk = per_w // SC_CHUNK

    @functools.partial(
        pl.kernel, mesh=_sc_mesh(), out_type=jax.ShapeDtypeStruct((b, d), table.dtype),
        scratch_types=[pltpu.VMEM((n_chunk, SC_CHUNK), jnp.int32), pltpu.VMEM((2, SC_CHUNK, d), table.dtype),
                       pltpu.SemaphoreType.DMA((2,))])
    def gather(table_hbm, idx_hbm, out_hbm, idx_v, rows_v, sem):
        wid = _sc_worker()
        pltpu.sync_copy(idx_hbm.at[wid], idx_v)

        def fetch(j, slot):
            return pltpu.make_async_copy(table_hbm.at[idx_v.at[j]], rows_v.at[slot], sem.at[slot])

        fetch(0, 0).start()

        @pl.loop(0, n_chunk, step=2)
        def _(j):
            for slot in range(2):
                jj = j + slot
                fetch(jj, slot).wait()

                @pl.when(jj + 1 < n_chunk)
                def _():
                    fetch(jj + 1, 1 - slot).start()

                off = pl.multiple_of(wid * per_w + jj * SC_CHUNK, SC_CHUNK)
                pltpu.sync_copy(rows_v.at[slot], out_hbm.at[pl.ds(off, SC_CHUNK)])

    return gather(table, idx.reshape(SC_WORKERS, n_chunk, SC_CHUNK))


def _expert_ffn_kernel(te_ref, nu_ref, xs_ref, wgu_ref, wd_ref, ys_ref, acc_ref, *, ck):
    del te_ref

    @pl.when(pl.program_id(0) < nu_ref[0])
    def _():
        acc_ref[...] = jnp.zeros_like(acc_ref)
        _swiglu_into(acc_ref, _unpack_halves(xs_ref[...]), wgu_ref.at[0], wd_ref.at[0], wd_ref.shape[1], ck)
        ys_ref[...] = _pack_halves(acc_ref[...])


def _expert_ffn(xs, tile_expert, n_used, wgu, wd, tr, name):
    n_rows, half = xs.shape
    _, d_ffe, d = wd.shape
    return pl.pallas_call(
        functools.partial(_expert_ffn_kernel, ck=256),
        grid_spec=pltpu.PrefetchScalarGridSpec(
            num_scalar_prefetch=2,
            grid=(n_rows // tr,),
            in_specs=[pl.BlockSpec((tr, half), lambda j, te, nu: (j, 0)),
                      pl.BlockSpec((1, d, 2 * d_ffe), lambda j, te, nu: (te[j], 0, 0)),
                      pl.BlockSpec((1, d_ffe, d), lambda j, te, nu: (te[j], 0, 0))],
            out_specs=pl.BlockSpec((tr, half), lambda j, te, nu: (j, 0)),
            scratch_shapes=[pltpu.VMEM((tr, d), F32)]),
        out_shape=jax.ShapeDtypeStruct((n_rows, half), jnp.int32),
        compiler_params=_params(1),
        name=name,
    )(tile_expert, n_used, xs, wgu, wd)


def _combine_kernel(h1_ref, y1_ref, y2_ref, gate_ref, p_ref, gn_ref, wg_ref, wp_ref, o_ref):
    gate = gate_ref[...]
    h2 = (h1_ref[...] + gate[:, 0:1] * _unpack_halves(y1_ref[...]).astype(F32)
          + gate[:, 1:2] * _unpack_halves(y2_ref[...]).astype(F32))
    o_ref[...] = _ple(h2, p_ref[...], gn_ref[...], wg_ref[...], wp_ref[...])


def _combine(h1, yg, gate, p, gn, wg, wp, tm, name):
    t, d = h1.shape
    n_tile = t // tm
    tile = lambda n: pl.BlockSpec((tm, n), lambda i: (i, 0))
    return pl.pallas_call(
        _combine_kernel, grid=(n_tile,),
        in_specs=[tile(d), tile(d // 2), pl.BlockSpec((tm, d // 2), lambda i: (i + n_tile, 0)),
                  tile(TOP_K), tile(p.shape[1]), _resident(gn.shape), _resident(wg.shape), _resident(wp.shape)],
        out_specs=tile(d),
        out_shape=jax.ShapeDtypeStruct((t, d), F32),
        compiler_params=_params(1),
        name=name,
    )(h1, yg, yg, gate, p, gn, wg, wp)


def _moe_post(h1, hnp, sel, gate, counts, p, wgu, wd, gn, wg, wp, tm, layer):
    t = h1.shape[0]
    n_exp = wd.shape[0]
    tr = EXPERT_ROW_TILE
    n_rows = TOP_K * t + n_exp * tr
    cnt = counts[0].astype(jnp.int32)
    padded = (cnt + tr - 1) // tr * tr
    ends = jnp.cumsum(padded)
    starts = ends - padded
    e12, rank = sel[:, :TOP_K], sel[:, TOP_K:]
    pos = jnp.sum(jnp.where(e12[:, :, None] == jnp.arange(n_exp), starts, 0), axis=-1) + rank
    pos = pos.T.reshape(-1)
    n_used = (ends[-1] // tr).reshape(1)
    tile_start = jnp.arange(n_rows // tr, dtype=jnp.int32) * tr
    tile_expert = jnp.minimum(jnp.sum(tile_start[:, None] >= ends[None, :], axis=-1), n_exp - 1)

    xs = _sc_scatter_rows(hnp, pos, n_rows)
    ys = _expert_ffn(xs, tile_expert.astype(jnp.int32), n_used.astype(jnp.int32), wgu, wd, tr, f"experts_{layer}")
    yg = _sc_gather_rows(ys, pos)
    return _combine(h1, yg, gate, p, gn, wg, wp, tm, f"combine_{layer}")


def kernel(x, p, norm_mix, norm_ffn, sb_w_qkv, sb_q_gain, sb_k_gain, sb_w_o, pool_w_in, pool_w_grp,
           pool_scale, hgrn_w_in, hgrn_lower_bounds, hgrn_o_gain, hgrn_w_o, ffn_w_gu, ffn_w_d,
           moe_router, moe_w_gu, moe_w_d, ple_w, ple_gate_norm, ple_gate_w):
    b, s, d = x.shape
    depth = p.shape[0]
    t = b * s
    tm = 512
    bf = lambda w: w.astype(BF16)
    vec = lambda g: g.reshape(1, -1)

    h = x.reshape(t, d)
    for i in range(depth):
        kind, j = i % N_MIXERS, i // N_MIXERS
        p_i = p[i].reshape(t, -1)
        ple = (vec(ple_gate_norm[i]), bf(ple_gate_w[i]), bf(ple_w[i]))
        y = w_o = None
        if kind == 0:
            qkv = _norm_matmul(h, vec(norm_mix[i]), bf(sb_w_qkv[j]), BF16, tm, f"sb_qkv_{i}")
            y = _sb_attention(qkv.reshape(b, s, 3 * d), sb_q_gain[j], sb_k_gain[j], f"sb_attn_{i}")
            y, w_o = y.reshape(t, d), bf(sb_w_o[j])
        elif kind == 1:
            h = _pool_mixer(h, s, vec(norm_mix[i]), bf(pool_w_in[j]), bf(pool_w_grp[j]),
                            vec(pool_scale[j]), tm, f"pool_{i}")
        else:
            proj = _norm_matmul(h, vec(norm_mix[i]), bf(hgrn_w_in[j]), BF16, tm, f"hgrn_in_{i}")
            y = _hgrn_core(proj.reshape(b, s, 4 * d), hgrn_lower_bounds, i, hgrn_o_gain[j], f"hgrn_{i}")
            y, w_o = y.reshape(t, d), bf(hgrn_w_o[j])

        c = i // 2
        if i % 2 == 0:
            h = _dense_post(h, y, p_i, w_o, vec(norm_ffn[i]), bf(ffn_w_gu[c]), bf(ffn_w_d[c]), *ple,
                            tm, f"dense_{i}")
        else:
            routed = _route(h, y, w_o, vec(norm_ffn[i]), bf(moe_router[c]), tm, f"route_{i}")
            h = _moe_post(*routed, p_i, bf(moe_w_gu[c]), bf(moe_w_d[c]), *ple, tm, i)
    return h.reshape(b, s, d)
```

```python
import functools

import jax
import jax.numpy as jnp
from jax import lax
from jax.experimental import pallas as pl
from jax.experimental.pallas import tpu as pltpu
from jax.experimental.pallas import tpu_sc as plsc

F32 = jnp.float32
BF16 = jnp.bfloat16

EPS = 1e-6
LOG2_E = 1.4426950408889634
N_MIXERS = 3
SB_HEAD_DIM = 64
POOL_WINDOWS = (2, 4, 8, 16)
HGRN_CHUNK = 128
TOP_K = 2

LANES = 128
SB_BLOCK = 256
POOL_HALO = 16
EXPERT_ROW_TILE = 512
SC_CORES = 2
SC_WORKERS = SC_CORES * 16
SC_CHUNK = 64
VMEM_LIMIT = 56 * 1024 * 1024
LOG_ZERO_F32 = -104.0


def _dot(a, b):
    return jnp.dot(a, b, preferred_element_type=F32)


def _dot_nt(a, b):
    return lax.dot_general(a, b, (((1,), (1,)), ((), ())), preferred_element_type=F32)


def _split_bf16(x):
    hi = x.astype(BF16)
    lo = (x - hi.astype(F32)).astype(BF16)
    return hi, lo


def _rms(x, g):
    ms = jnp.mean(x * x, axis=-1, keepdims=True)
    return x * lax.rsqrt(ms + EPS) * g


def _log_sigmoid(x):
    return -(jnp.maximum(-x, 0.0) + jnp.log(1.0 + jnp.exp(-jnp.abs(x))))


def _silu(x):
    return x * jax.nn.sigmoid(x)


def _params(n_axes):
    return pltpu.CompilerParams(
        dimension_semantics=("arbitrary",) * n_axes, vmem_limit_bytes=VMEM_LIMIT)


def _resident(shape):
    nd = len(shape)
    return pl.BlockSpec(shape, lambda *_: (0,) * nd)


def _norm_matmul_kernel(h_ref, g_ref, w_ref, o_ref, *, n_chunk):
    xn = _rms(h_ref[...], g_ref[...]).astype(BF16)
    for c0 in range(0, w_ref.shape[1], n_chunk):
        o_ref[:, c0:c0 + n_chunk] = _dot(xn, w_ref[:, c0:c0 + n_chunk]).astype(o_ref.dtype)


def _norm_matmul(h, g, w, out_dtype, tm, name):
    t, d = h.shape
    n = w.shape[1]
    return pl.pallas_call(
        functools.partial(_norm_matmul_kernel, n_chunk=512),
        grid=(t // tm,),
        in_specs=[pl.BlockSpec((tm, d), lambda i: (i, 0)), _resident((1, d)), _resident((d, n))],
        out_specs=pl.BlockSpec((tm, n), lambda i: (i, 0)),
        out_shape=jax.ShapeDtypeStruct((t, n), out_dtype),
        compiler_params=_params(1),
        name=name,
    )(h, g, w)


def _sb_attn_kernel(q_ref, k_ref, v_ref, qg_ref, kg_ref, o_ref,
                    qn_s, kcat_s, vcat_s, oacc_s, carry_s, *, blk):
    s_len = q_ref.shape[1]
    n_blk = s_len // blk
    scale = SB_HEAD_DIM ** -0.5

    row = lax.broadcasted_iota(jnp.int32, (LANES, LANES), 0)
    col = lax.broadcasted_iota(jnp.int32, (LANES, LANES), 1)
    head_sum = jnp.where((row >> 6) == (col >> 6), 1.0, 0.0).astype(BF16)
    head0 = lax.broadcasted_iota(jnp.int32, (blk, LANES), 1) < SB_HEAD_DIM

    def head_norm(x, gain):
        hi, lo = _split_bf16(x * x)
        ss = _dot(hi, head_sum) + _dot(lo, head_sum)
        return x * lax.rsqrt(ss * (1.0 / SB_HEAD_DIM) + EPS) * gain

    def prep(c, _):
        r = pl.ds(pl.multiple_of(c * blk, blk), blk)
        qn = head_norm(q_ref[0, r, :].astype(F32), qg_ref[...] * scale)
        qn_s[r, :] = qn.astype(BF16)
        kn = head_norm(k_ref[0, r, :].astype(F32), kg_ref[...]).astype(BF16)
        vv = v_ref[0, r, :]
        zero = jnp.zeros_like(kn)
        kcat_s[c] = jnp.concatenate([jnp.where(head0, kn, zero), jnp.where(head0, zero, kn)], axis=0)
        vcat_s[c] = jnp.concatenate([jnp.where(head0, vv, zero), jnp.where(head0, zero, vv)], axis=0)
        return 0

    lax.fori_loop(0, n_blk, prep, 0, unroll=2)

    r2 = lax.broadcasted_iota(jnp.int32, (blk, blk), 0)
    c2 = lax.broadcasted_iota(jnp.int32, (blk, blk), 1)
    cum_mat = jnp.where(r2 > c2, 1.0, 0.0).astype(BF16)
    causal = (lax.broadcasted_iota(jnp.int32, (blk, 2 * blk), 1) & (blk - 1)) < \
        lax.broadcasted_iota(jnp.int32, (blk, 2 * blk), 0)

    def step(q_blk, kb, carry, oacc, diagonal):
        z = _dot_nt(q_blk, kcat_s[kb])
        neg_log_keep = jnp.maximum(z, 0.0) + jnp.log(1.0 + jnp.exp(-jnp.abs(z)))
        if diagonal:
            neg_log_keep = jnp.where(causal, neg_log_keep, 0.0)
        a_heads, new_carry = [], []
        for hd in range(2):
            cols = slice(hd * blk, (hd + 1) * blk)
            nlk = neg_log_keep[:, cols]
            later = _dot(nlk.astype(BF16), cum_mat)
            a = jnp.exp(z[:, cols] - nlk - later - carry[hd])
            if diagonal:
                a = jnp.where(causal[:, cols], a, 0.0)
            a_heads.append(a.astype(BF16))
            new_carry.append(carry[hd] + (later[:, 0:1] + nlk[:, 0:1]))
        oacc = oacc + _dot(jnp.concatenate(a_heads, axis=1), vcat_s[kb])
        return new_carry, oacc

    rows = lambda qb: pl.ds(pl.multiple_of(qb * blk, blk), blk)
    zero_carry = [jnp.zeros((blk, 1), F32)] * 2
    zero_acc = jnp.zeros((blk, LANES), F32)

    def head_blocks(qb, slot):
        q_blk = qn_s[rows(qb), :]
        carry, oacc = step(q_blk, qb, zero_carry, zero_acc, True)
        carry, oacc = step(q_blk, qb - 1, carry, oacc, False)
        carry_s[slot, 0], carry_s[slot, 1] = carry
        oacc_s[slot] = oacc
        return q_blk, jnp.min(jnp.minimum(carry[0], carry[1]))

    def tail_blocks(qb, slot, q_blk, low):
        def kv_cond(st):
            kb, low = st
            return jnp.logical_and(kb >= 0, low < -LOG_ZERO_F32)

        def kv_step(st):
            kb, _ = st
            carry, oacc = step(q_blk, kb, [carry_s[slot, 0], carry_s[slot, 1]], oacc_s[slot], False)
            carry_s[slot, 0], carry_s[slot, 1] = carry
            oacc_s[slot] = oacc
            return kb - 1, jnp.min(jnp.minimum(carry[0], carry[1]))

        lax.while_loop(kv_cond, kv_step, (qb - 2, low))
        o_ref[0, rows(qb), :] = oacc_s[slot].astype(o_ref.dtype)

    _, oacc0 = step(qn_s[rows(0), :], 0, zero_carry, zero_acc, True)
    o_ref[0, rows(0), :] = oacc0.astype(o_ref.dtype)

    def q_pair(i, _):
        qa = 1 + 2 * i
        blk_a, low_a = head_blocks(qa, 0)
        blk_b, low_b = head_blocks(qa + 1, 1)
        tail_blocks(qa, 0, blk_a, low_a)
        tail_blocks(qa + 1, 1, blk_b, low_b)
        return 0

    n_pair_trips = (n_blk - 1) // 2
    lax.fori_loop(0, n_pair_trips, q_pair, 0)
    if (n_blk - 1) % 2:
        last = n_blk - 1
        blk_l, low_l = head_blocks(last, 0)
        tail_blocks(last, 0, blk_l, low_l)


def _sb_attention(qkv, q_gain, k_gain, name):
    b, s, d3 = qkv.shape
    d = d3 // 3
    n_pair = d // LANES
    blk = SB_BLOCK
    n_blk = s // blk
    gains = [jnp.tile(g.astype(F32), LANES // SB_HEAD_DIM).reshape(1, LANES) for g in (q_gain, k_gain)]
    col_block = lambda off: pl.BlockSpec((1, s, LANES), lambda bi, hp: (bi, 0, off + hp))
    return pl.pallas_call(
        functools.partial(_sb_attn_kernel, blk=blk),
        grid=(b, n_pair),
        in_specs=[col_block(0), col_block(n_pair), col_block(2 * n_pair),
                  _resident((1, LANES)), _resident((1, LANES))],
        out_specs=pl.BlockSpec((1, s, LANES), lambda bi, hp: (bi, 0, hp)),
        out_shape=jax.ShapeDtypeStruct((b, s, d), BF16),
        scratch_shapes=[
            pltpu.VMEM((s, LANES), BF16),
            pltpu.VMEM((n_blk, 2 * blk, LANES), BF16),
            pltpu.VMEM((n_blk, 2 * blk, LANES), BF16),
            pltpu.VMEM((2, blk, LANES), F32),
            pltpu.VMEM((2, 2, blk, 1), F32),
        ],
        compiler_params=_params(2),
        name=name,
    )(qkv, qkv, qkv, *gains)


def _pool_kernel(h_ref, g_ref, win_ref, wgrp_ref, sc_ref, o_ref, ext_s, *, tm, seq):
    i = pl.program_id(0)
    t0 = (i * tm) % seq

    @pl.when(t0 == 0)
    def _():
        ext_s[0:POOL_HALO, :] = jnp.zeros((POOL_HALO, ext_s.shape[1]), F32)

    x = h_ref[...]
    u = _dot(_rms(x, g_ref[...]).astype(BF16), win_ref[...])
    ext_s[POOL_HALO:POOL_HALO + tm, :] = u

    pos = t0 + lax.broadcasted_iota(jnp.int32, (tm, 1), 0)
    gdim = u.shape[1] // len(POOL_WINDOWS)
    for g, w in enumerate(POOL_WINDOWS):
        cols = slice(g * gdim, (g + 1) * gdim)
        ug = u[:, cols]
        win_sum = ug
        for back in range(1, w):
            win_sum = win_sum + ext_s[POOL_HALO - back:POOL_HALO - back + tm, cols]
        cnt = jnp.minimum(pos + 1, w).astype(F32)
        pooled = win_sum / cnt - ug
        y = _dot(pooled.astype(BF16), wgrp_ref[g]) * sc_ref[:, cols]
        o_ref[:, cols] = x[:, cols] + y

    ext_s[0:POOL_HALO, :] = ext_s[tm:tm + POOL_HALO, :]


def _pool_mixer(h, seq, g, w_in, w_grp, scale, tm, name):
    t, d = h.shape
    ng, gd, _ = w_grp.shape
    return pl.pallas_call(
        functools.partial(_pool_kernel, tm=tm, seq=seq),
        grid=(t // tm,),
        in_specs=[pl.BlockSpec((tm, d), lambda i: (i, 0)), _resident((1, d)), _resident((d, d)),
                  _resident((ng, gd, gd)), _resident((1, d))],
        out_specs=pl.BlockSpec((tm, d), lambda i: (i, 0)),
        out_shape=jax.ShapeDtypeStruct((t, d), F32),
        scratch_shapes=[pltpu.VMEM((POOL_HALO + tm, d), F32)],
        compiler_params=_params(1),
        name=name,
    )(h, g, w_in, w_grp, scale)


_HGRN_LEVELS = (64, 32, 16, 8, 4, 2, 1)


def _hgrn_kernel(q_ref, f_ref, i_ref, g_ref, lbp_ref, og_ref, o_ref, cum_s, level_s, state_s, *,
                 layer_idx):
    c = HGRN_CHUNK
    n_chunk = q_ref.shape[1] // c

    lbp = lbp_ref[...]
    e = jnp.exp(lbp - jnp.max(lbp, axis=0, keepdims=True))
    sm = e / jnp.sum(e, axis=0, keepdims=True)
    lb = jnp.sum(sm[1:layer_idx + 1], axis=0, keepdims=True)
    log_lb = jnp.log(lb)
    log_1m_lb = jnp.log1p(-lb)

    r = lax.broadcasted_iota(jnp.int32, (c, c), 0)
    s = lax.broadcasted_iota(jnp.int32, (c, c), 1)
    one = lambda m: jnp.where(m, 1.0, 0.0).astype(BF16)
    cum_s[0:c, :] = one(s <= r)
    cum_s[c:2 * c, :] = one(s > r)
    for l, m in enumerate(_HGRN_LEVELS):
        sh = m.bit_length() - 1
        same = (r >> sh) == (s >> sh)
        odd = ((r >> sh) & 1) == 1
        cum_s[(2 + l) * c:(3 + l) * c, :] = one(same & (odd == (s <= r)))

    level = jnp.full((c, c), len(_HGRN_LEVELS), jnp.int32)
    for l, m in enumerate(_HGRN_LEVELS):
        sh = m.bit_length() - 1
        level = jnp.where((((r >> sh) & 1) == 1) & ((s >> sh) == (r >> sh) - 1), l, level)
    level_s[...] = level

    state_s[...] = jnp.zeros_like(state_s)

    def chunk(n, _):
        rows = pl.ds(pl.multiple_of(n * c, c), c)
        fg = f_ref[0, rows, :].astype(F32)
        a_ = log_lb
        c_ = log_1m_lb + _log_sigmoid(fg)
        log_f = jnp.maximum(a_, c_) + jnp.log(1.0 + jnp.exp(-jnp.abs(a_ - c_)))
        k = (1.0 - lb) * jax.nn.sigmoid(-fg)
        q = _silu(q_ref[0, rows, :].astype(F32))
        v = i_ref[0, rows, :]

        hi, lo = _split_bf16(log_f * LOG2_E)
        cum = _dot(cum_s[...], hi) + _dot(cum_s[...], lo)
        dec = jnp.exp2(cum)
        b_end = dec[c - 1:c, :]

        state = state_s[...]
        o = _dot_nt((q * dec[0:c]).astype(BF16), state.astype(BF16))
        kd = (k * dec[c:2 * c]).astype(BF16)
        v_t = v.astype(F32).T.astype(BF16)
        state_s[...] = state * b_end + _dot(v_t, kd)

        scores = jnp.zeros((c, c), F32)
        level = level_s[...]
        for l in range(len(_HGRN_LEVELS)):
            d_l = dec[(2 + l) * c:(3 + l) * c]
            sc = _dot_nt((q * d_l).astype(BF16), (k * d_l).astype(BF16))
            scores = jnp.where(level == l, sc, scores)
        o = o + _dot(scores.astype(BF16), v)
        o = o + jnp.sum(q * k, axis=-1, keepdims=True) * v.astype(F32)

        o = _rms(o, og_ref[...]) * _silu(g_ref[0, rows, :].astype(F32))
        o_ref[0, rows, :] = o.astype(o_ref.dtype)
        return 0

    lax.fori_loop(0, n_chunk, chunk, 0, unroll=8)


def _hgrn_core(proj, lower_bounds, layer_idx, o_gain, name):
    b, s, d4 = proj.shape
    d = d4 // 4
    n_head = d // LANES
    depth = lower_bounds.shape[0]
    col_block = lambda off: pl.BlockSpec((1, s, LANES), lambda bi, hd: (bi, 0, off + hd))
    return pl.pallas_call(
        functools.partial(_hgrn_kernel, layer_idx=layer_idx),
        grid=(b, n_head),
        in_specs=[col_block(0), col_block(n_head), col_block(2 * n_head), col_block(3 * n_head),
                  pl.BlockSpec((depth, LANES), lambda bi, hd: (0, hd)), _resident((1, LANES))],
        out_specs=pl.BlockSpec((1, s, LANES), lambda bi, hd: (bi, 0, hd)),
        out_shape=jax.ShapeDtypeStruct((b, s, d), BF16),
        scratch_shapes=[pltpu.VMEM(((2 + len(_HGRN_LEVELS)) * HGRN_CHUNK, HGRN_CHUNK), BF16),
                        pltpu.VMEM((HGRN_CHUNK, HGRN_CHUNK), jnp.int32),
                        pltpu.VMEM((LANES, LANES), F32)],
        compiler_params=_params(2),
        name=name,
    )(proj, proj, proj, proj, lower_bounds, o_gain.reshape(1, LANES))


def _swiglu_into(acc_ref, hn, wgu, wd, d_ff, ck, row_scale=None):
    for c0 in range(0, d_ff, ck):
        c1 = min(c0 + ck, d_ff)
        gate = _dot(hn, wgu[:, c0:c1])
        up = _dot(hn, wgu[:, d_ff + c0:d_ff + c1])
        act = _silu(gate) * up
        if row_scale is not None:
            act = act * row_scale
        acc_ref[...] += _dot(act.astype(BF16), wd[c0:c1, :])


def _ple(h, p, gn, wg, wp):
    gate = jax.nn.sigmoid(_dot(_rms(h, gn).astype(BF16), wg))
    return h + gate * _dot(p.astype(BF16), wp)


def _dense_post_kernel(h_ref, y_ref, p_ref, wo_ref, nf_ref, wgu_ref, wd_ref, gn_ref, wg_ref, wp_ref,
                       o_ref, acc_ref, *, ck):
    h1 = h_ref[...] + _dot(y_ref[...], wo_ref[...])
    acc_ref[...] = h1
    hn = _rms(h1, nf_ref[...]).astype(BF16)
    _swiglu_into(acc_ref, hn, wgu_ref, wd_ref, wd_ref.shape[0], ck)
    o_ref[...] = _ple(acc_ref[...], p_ref[...], gn_ref[...], wg_ref[...], wp_ref[...])


def _dense_post(h, y, p, wo, nf, wgu, wd, gn, wg, wp, tm, name):
    t, d = h.shape
    tile = lambda n: pl.BlockSpec((tm, n), lambda i: (i, 0))
    weights = (wo, nf, wgu, wd, gn, wg, wp)
    return pl.pallas_call(
        functools.partial(_dense_post_kernel, ck=256),
        grid=(t // tm,),
        in_specs=[tile(d), tile(d), tile(p.shape[1])] + [_resident(w.shape) for w in weights],
        out_specs=tile(d),
        out_shape=jax.ShapeDtypeStruct((t, d), F32),
        scratch_shapes=[pltpu.VMEM((tm, d), F32)],
        compiler_params=_params(1),
        name=name,
    )(h, y, p, *weights)


def _pack_halves(x):
    n = x.shape[1] // 2
    bits = lambda v: lax.bitcast_convert_type(v.astype(BF16).astype(F32), jnp.int32)
    return bits(x[:, n:]) | lax.shift_right_logical(bits(x[:, :n]), 16)


def _unpack_halves(u):
    lo = lax.bitcast_convert_type(lax.shift_left(u, 16), F32)
    hi = lax.bitcast_convert_type(u & jnp.int32(-65536), F32)
    return jnp.concatenate([lo.astype(BF16), hi.astype(BF16)], axis=1)


def _route_tail(h1, nf_ref, wr_ref, cnt_s, hnp_ref, sel_ref, gate_ref, cnt_ref):
    @pl.when(pl.program_id(0) == 0)
    def _():
        cnt_s[...] = jnp.zeros_like(cnt_s)

    hn = _rms(h1, nf_ref[...]).astype(BF16)
    hnp_ref[...] = _pack_halves(hn)
    logits = _dot(hn, wr_ref[...])
    tm, n_exp = logits.shape
    idx = lax.broadcasted_iota(jnp.int32, logits.shape, 1)
    m1 = jnp.max(logits, axis=-1, keepdims=True)
    i1 = jnp.min(jnp.where(logits == m1, idx, n_exp), axis=-1, keepdims=True)
    rest = jnp.where(idx == i1, -jnp.inf, logits)
    m2 = jnp.max(rest, axis=-1, keepdims=True)
    i2 = jnp.min(jnp.where(rest == m2, idx, n_exp), axis=-1, keepdims=True)
    e2 = jnp.exp(m2 - m1)
    g1 = 1.0 / (1.0 + e2)

    onehot = jnp.where(idx == i1, 1.0, jnp.where(idx == i2, 1.0, 0.0))
    r = lax.broadcasted_iota(jnp.int32, (tm, tm), 0)
    c = lax.broadcasted_iota(jnp.int32, (tm, tm), 1)
    before = cnt_s[...] + _dot(jnp.where(c < r, 1.0, 0.0).astype(BF16), onehot.astype(BF16))
    rank1 = jnp.sum(jnp.where(idx == i1, before, 0.0), axis=-1, keepdims=True).astype(jnp.int32)
    rank2 = jnp.sum(jnp.where(idx == i2, before, 0.0), axis=-1, keepdims=True).astype(jnp.int32)
    cnt_s[...] += jnp.sum(onehot, axis=0, keepdims=True)
    cnt_ref[...] = cnt_s[...]

    lane = lax.broadcasted_iota(jnp.int32, sel_ref.shape, 1)
    sel_ref[...] = jnp.where(lane == 0, i1, jnp.where(lane == 1, i2, jnp.where(lane == 2, rank1, rank2)))
    gate_ref[...] = jnp.where(lax.broadcasted_iota(jnp.int32, gate_ref.shape, 1) == 0, g1, e2 * g1)


def _route_proj_kernel(h_ref, y_ref, wo_ref, nf_ref, wr_ref, h1_ref, hnp_ref, sel_ref, gate_ref, cnt_ref,
                       cnt_s):
    h1 = h_ref[...] + _dot(y_ref[...], wo_ref[...])
    h1_ref[...] = h1
    _route_tail(h1, nf_ref, wr_ref, cnt_s, hnp_ref, sel_ref, gate_ref, cnt_ref)


def _route_kernel(h_ref, nf_ref, wr_ref, hnp_ref, sel_ref, gate_ref, cnt_ref, cnt_s):
    _route_tail(h_ref[...], nf_ref, wr_ref, cnt_s, hnp_ref, sel_ref, gate_ref, cnt_ref)


def _route(h, y, wo, nf, wr, tm, name):
    t, d = h.shape
    n_exp = wr.shape[1]
    tile = lambda n: pl.BlockSpec((tm, n), lambda i: (i, 0))
    outs = [jax.ShapeDtypeStruct((t, d // 2), jnp.int32), jax.ShapeDtypeStruct((t, 4), jnp.int32),
            jax.ShapeDtypeStruct((t, TOP_K), F32), jax.ShapeDtypeStruct((1, n_exp), F32)]
    out_specs = [tile(d // 2), tile(4), tile(TOP_K), _resident((1, n_exp))]
    scratch = [pltpu.VMEM((1, n_exp), F32)]
    if y is None:
        res = pl.pallas_call(
            _route_kernel, grid=(t // tm,),
            in_specs=[tile(d), _resident(nf.shape), _resident(wr.shape)],
            out_specs=out_specs, out_shape=outs, scratch_shapes=scratch,
            compiler_params=_params(1), name=name,
        )(h, nf, wr)
        return (h, *res)
    return pl.pallas_call(
        _route_proj_kernel, grid=(t // tm,),
        in_specs=[tile(d), tile(d), _resident(wo.shape), _resident(nf.shape), _resident(wr.shape)],
        out_specs=[tile(d)] + out_specs,
        out_shape=[jax.ShapeDtypeStruct((t, d), F32)] + outs, scratch_shapes=scratch,
        compiler_params=_params(1), name=name,
    )(h, y, wo, nf, wr)


def _sc_mesh():
    return plsc.VectorSubcoreMesh(core_axis_name="c", subcore_axis_name="s")


def _sc_worker():
    return lax.axis_index("s") * SC_CORES + lax.axis_index("c")


def _sc_scatter_rows(x, idx, n_out):
    r, d = x.shape
    per_w = idx.shape[0] // SC_WORKERS
    n_chunk = per_w // SC_CHUNK

    @functools.partial(
        pl.kernel, mesh=_sc_mesh(), out_type=jax.ShapeDtypeStruct((n_out, d), x.dtype),
        scratch_types=[pltpu.VMEM((SC_CHUNK,), jnp.int32), pltpu.VMEM((SC_CHUNK, d), x.dtype),
                       pltpu.SemaphoreType.DMA])
    def scatter(x_hbm, idx_hbm, out_hbm, idx_v, rows_v, sem):
        base = _sc_worker() * per_w

        @pl.loop(0, n_chunk)
        def _(j):
            off = pl.multiple_of(base + j * SC_CHUNK, SC_CHUNK)
            src = pl.multiple_of(off % r, SC_CHUNK)
            pltpu.sync_copy(idx_hbm.at[pl.ds(off, SC_CHUNK)], idx_v)
            pltpu.sync_copy(x_hbm.at[pl.ds(src, SC_CHUNK)], rows_v)
            pltpu.async_copy(rows_v, out_hbm.at[idx_v], sem).wait()

    return scatter(x, idx)


def _sc_gather_rows(table, idx):
    d = table.shape[1]
    b = idx.shape[0]
    per_w = b // SC_WORKERS
    n_chunk = per_w // SC_CHUNK

    @functools.partial(
        pl.kernel, mesh=_sc_mesh(), out_type=jax.ShapeDtypeStruct((b, d), table.dtype),
        scratch_types=[pltpu.VMEM((n_chunk, SC_CHUNK), jnp.int32), pltpu.VMEM((2, SC_CHUNK, d), table.dtype),
                       pltpu.SemaphoreType.DMA((2,))])
    def gather(table_hbm, idx_hbm, out_hbm, idx_v, rows_v, sem):
        wid = _sc_worker()
        pltpu.sync_copy(idx_hbm.at[wid], idx_v)

        def fetch(j, slot):
            return pltpu.make_async_copy(table_hbm.at[idx_v.at[j]], rows_v.at[slot], sem.at[slot])

        fetch(0, 0).start()

        @pl.loop(0, n_chunk, step=2)
        def _(j):
            for slot in range(2):
                jj = j + slot
                fetch(jj, slot).wait()

                @pl.when(jj + 1 < n_chunk)
                def _():
                    fetch(jj + 1, 1 - slot).start()

                off = pl.multiple_of(wid * per_w + jj * SC_CHUNK, SC_CHUNK)
                pltpu.sync_copy(rows_v.at[slot], out_hbm.at[pl.ds(off, SC_CHUNK)])

    return gather(table, idx.reshape(SC_WORKERS, n_chunk, SC_CHUNK))


def _expert_ffn_kernel(te_ref, nu_ref, xs_ref, wgu_ref, wd_ref, ys_ref, acc_ref, *, ck):
    del te_ref

    @pl.when(pl.program_id(0) < nu_ref[0])
    def _():
        acc_ref[...] = jnp.zeros_like(acc_ref)
        _swiglu_into(acc_ref, _unpack_halves(xs_ref[...]), wgu_ref.at[0], wd_ref.at[0], wd_ref.shape[1], ck)
        ys_ref[...] = _pack_halves(acc_ref[...])


def _expert_ffn(xs, tile_expert, n_used, wgu, wd, tr, name):
    n_rows, half = xs.shape
    _, d_ffe, d = wd.shape
    return pl.pallas_call(
        functools.partial(_expert_ffn_kernel, ck=256),
        grid_spec=pltpu.PrefetchScalarGridSpec(
            num_scalar_prefetch=2,
            grid=(n_rows // tr,),
            in_specs=[pl.BlockSpec((tr, half), lambda j, te, nu: (j, 0)),
                      pl.BlockSpec((1, d, 2 * d_ffe), lambda j, te, nu: (te[j], 0, 0)),
                      pl.BlockSpec((1, d_ffe, d), lambda j, te, nu: (te[j], 0, 0))],
            out_specs=pl.BlockSpec((tr, half), lambda j, te, nu: (j, 0)),
            scratch_shapes=[pltpu.VMEM((tr, d), F32)]),
        out_shape=jax.ShapeDtypeStruct((n_rows, half), jnp.int32),
        compiler_params=_params(1),
        name=name,
    )(tile_expert, n_used, xs, wgu, wd)


def _combine_kernel(h1_ref, y1_ref, y2_ref, gate_ref, p_ref, gn_ref, wg_ref, wp_ref, o_ref):
    gate = gate_ref[...]
    h2 = (h1_ref[...] + gate[:, 0:1] * _unpack_halves(y1_ref[...]).astype(F32)
          + gate[:, 1:2] * _unpack_halves(y2_ref[...]).astype(F32))
    o_ref[...] = _ple(h2, p_ref[...], gn_ref[...], wg_ref[...], wp_ref[...])


def _combine(h1, yg, gate, p, gn, wg, wp, tm, name):
    t, d = h1.shape
    n_tile = t // tm
    tile = lambda n: pl.BlockSpec((tm, n), lambda i: (i, 0))
    return pl.pallas_call(
        _combine_kernel, grid=(n_tile,),
        in_specs=[tile(d), tile(d // 2), pl.BlockSpec((tm, d // 2), lambda i: (i + n_tile, 0)),
                  tile(TOP_K), tile(p.shape[1]), _resident(gn.shape), _resident(wg.shape), _resident(wp.shape)],
        out_specs=tile(d),
        out_shape=jax.ShapeDtypeStruct((t, d), F32),
        compiler_params=_params(1),
        name=name,
    )(h1, yg, yg, gate, p, gn, wg, wp)


def _moe_post(h1, hnp, sel, gate, counts, p, wgu, wd, gn, wg, wp, tm, layer):
    t = h1.shape[0]
    n_exp = wd.shape[0]
    tr = EXPERT_ROW_TILE
    n_rows = TOP_K * t + n_exp * tr
    cnt = counts[0].astype(jnp.int32)
    padded = (cnt + tr - 1) // tr * tr
    ends = jnp.cumsum(padded)
    starts = ends - padded
    e12, rank = sel[:, :TOP_K], sel[:, TOP_K:]
    pos = jnp.sum(jnp.where(e12[:, :, None] == jnp.arange(n_exp), starts, 0), axis=-1) + rank
    pos = pos.T.reshape(-1)
    n_used = (ends[-1] // tr).reshape(1)
    tile_start = jnp.arange(n_rows // tr, dtype=jnp.int32) * tr
    tile_expert = jnp.minimum(jnp.sum(tile_start[:, None] >= ends[None, :], axis=-1), n_exp - 1)

    xs = _sc_scatter_rows(hnp, pos, n_rows)
    ys = _expert_ffn(xs, tile_expert.astype(jnp.int32), n_used.astype(jnp.int32), wgu, wd, tr, f"experts_{layer}")
    yg = _sc_gather_rows(ys, pos)
    return _combine(h1, yg, gate, p, gn, wg, wp, tm, f"combine_{layer}")


def kernel(x, p, norm_mix, norm_ffn, sb_w_qkv, sb_q_gain, sb_k_gain, sb_w_o, pool_w_in, pool_w_grp,
           pool_scale, hgrn_w_in, hgrn_lower_bounds, hgrn_o_gain, hgrn_w_o, ffn_w_gu, ffn_w_d,
           moe_router, moe_w_gu, moe_w_d, ple_w, ple_gate_norm, ple_gate_w):
    b, s, d = x.shape
    depth = p.shape[0]
    t = b * s
    tm = 512
    bf = lambda w: w.astype(BF16)
    vec = lambda g: g.reshape(1, -1)

    h = x.reshape(t, d)
    for i in range(depth):
        kind, j = i % N_MIXERS, i // N_MIXERS
        p_i = p[i].reshape(t, -1)
        ple = (vec(ple_gate_norm[i]), bf(ple_gate_w[i]), bf(ple_w[i]))
        y = w_o = None
        if kind == 0:
            qkv = _norm_matmul(h, vec(norm_mix[i]), bf(sb_w_qkv[j]), BF16, tm, f"sb_qkv_{i}")
            y = _sb_attention(qkv.reshape(b, s, 3 * d), sb_q_gain[j], sb_k_gain[j], f"sb_attn_{i}")
            y, w_o = y.reshape(t, d), bf(sb_w_o[j])
        elif kind == 1:
            h = _pool_mixer(h, s, vec(norm_mix[i]), bf(pool_w_in[j]), bf(pool_w_grp[j]),
                            vec(pool_scale[j]), tm, f"pool_{i}")
        else:
            proj = _norm_matmul(h, vec(norm_mix[i]), bf(hgrn_w_in[j]), BF16, tm, f"hgrn_in_{i}")
            y = _hgrn_core(proj.reshape(b, s, 4 * d), hgrn_lower_bounds, i, hgrn_o_gain[j], f"hgrn_{i}")
            y, w_o = y.reshape(t, d), bf(hgrn_w_o[j])

        c = i // 2
        if i % 2 == 0:
            h = _dense_post(h, y, p_i, w_o, vec(norm_ffn[i]), bf(ffn_w_gu[c]), bf(ffn_w_d[c]), *ple,
                            tm, f"dense_{i}")
        else:
            routed = _route(h, y, w_o, vec(norm_ffn[i]), bf(moe_router[c]), tm, f"route_{i}")
            h = _moe_post(*routed, p_i, bf(moe_w_gu[c]), bf(moe_w_d[c]), *ple, tm, i)
    return h.reshape(b, s, d)
```

```python
import functools

import jax
import jax.numpy as jnp
from jax import lax
from jax.experimental import pallas as pl
from jax.experimental.pallas import tpu as pltpu
from jax.experimental.pallas import tpu_sc as plsc

F32 = jnp.float32
BF16 = jnp.bfloat16

EPS = 1e-6
LOG2_E = 1.4426950408889634
N_MIXERS = 3
SB_HEAD_DIM = 64
POOL_WINDOWS = (2, 4, 8, 16)
HGRN_CHUNK = 128
TOP_K = 2

LANES = 128
SB_BLOCK = 256
POOL_HALO = 16
EXPERT_ROW_TILE = 512
SC_CORES = 2
SC_WORKERS = SC_CORES * 16
SC_CHUNK = 64
VMEM_LIMIT = 56 * 1024 * 1024
LOG_ZERO_F32 = -104.0


def _dot(a, b):
    return jnp.dot(a, b, preferred_element_type=F32)


def _dot_nt(a, b):
    return lax.dot_general(a, b, (((1,), (1,)), ((), ())), preferred_element_type=F32)


def _split_bf16(x):
    hi = x.astype(BF16)
    lo = (x - hi.astype(F32)).astype(BF16)
    return hi, lo


def _rms(x, g):
    ms = jnp.mean(x * x, axis=-1, keepdims=True)
    return x * lax.rsqrt(ms + EPS) * g


def _log_sigmoid(x):
    return -(jnp.maximum(-x, 0.0) + jnp.log(1.0 + jnp.exp(-jnp.abs(x))))


def _silu(x):
    return x * jax.nn.sigmoid(x)


def _params(n_axes):
    return pltpu.CompilerParams(
        dimension_semantics=("arbitrary",) * n_axes, vmem_limit_bytes=VMEM_LIMIT)


def _resident(shape):
    nd = len(shape)
    return pl.BlockSpec(shape, lambda *_: (0,) * nd)


def _norm_matmul_kernel(h_ref, g_ref, w_ref, o_ref, *, n_chunk):
    xn = _rms(h_ref[...], g_ref[...]).astype(BF16)
    for c0 in range(0, w_ref.shape[1], n_chunk):
        o_ref[:, c0:c0 + n_chunk] = _dot(xn, w_ref[:, c0:c0 + n_chunk]).astype(o_ref.dtype)


def _norm_matmul(h, g, w, out_dtype, tm, name):
    t, d = h.shape
    n = w.shape[1]
    return pl.pallas_call(
        functools.partial(_norm_matmul_kernel, n_chunk=512),
        grid=(t // tm,),
        in_specs=[pl.BlockSpec((tm, d), lambda i: (i, 0)), _resident((1, d)), _resident((d, n))],
        out_specs=pl.BlockSpec((tm, n), lambda i: (i, 0)),
        out_shape=jax.ShapeDtypeStruct((t, n), out_dtype),
        compiler_params=_params(1),
        name=name,
    )(h, g, w)


def _sb_attn_kernel(q_ref, k_ref, v_ref, qg_ref, kg_ref, o_ref,
                    qn_s, kcat_s, vcat_s, oacc_s, carry_s, *, blk):
    s_len = q_ref.shape[1]
    n_blk = s_len // blk
    scale = SB_HEAD_DIM ** -0.5

    row = lax.broadcasted_iota(jnp.int32, (LANES, LANES), 0)
    col = lax.broadcasted_iota(jnp.int32, (LANES, LANES), 1)
    head_sum = jnp.where((row >> 6) == (col >> 6), 1.0, 0.0).astype(BF16)
    head0 = lax.broadcasted_iota(jnp.int32, (blk, LANES), 1) < SB_HEAD_DIM

    def head_norm(x, gain):
        hi, lo = _split_bf16(x * x)
        ss = _dot(hi, head_sum) + _dot(lo, head_sum)
        return x * lax.rsqrt(ss * (1.0 / SB_HEAD_DIM) + EPS) * gain

    def prep(c, _):
        r = pl.ds(pl.multiple_of(c * blk, blk), blk)
        qn = head_norm(q_ref[0, r, :].astype(F32), qg_ref[...] * scale)
        qn_s[r, :] = qn.astype(BF16)
        kn = head_norm(k_ref[0, r, :].astype(F32), kg_ref[...]).astype(BF16)
        vv = v_ref[0, r, :]
        zero = jnp.zeros_like(kn)
        kcat_s[c] = jnp.concatenate([jnp.where(head0, kn, zero), jnp.where(head0, zero, kn)], axis=0)
        vcat_s[c] = jnp.concatenate([jnp.where(head0, vv, zero), jnp.where(head0, zero, vv)], axis=0)
        return 0

    lax.fori_loop(0, n_blk, prep, 0, unroll=2)

    r2 = lax.broadcasted_iota(jnp.int32, (blk, blk), 0)
    c2 = lax.broadcasted_iota(jnp.int32, (blk, blk), 1)
    cum_mat = jnp.where(r2 > c2, 1.0, 0.0).astype(BF16)
    causal = (lax.broadcasted_iota(jnp.int32, (blk, 2 * blk), 1) & (blk - 1)) < \
        lax.broadcasted_iota(jnp.int32, (blk, 2 * blk), 0)

    def step(q_blk, kb, carry, oacc, diagonal):
        z = _dot_nt(q_blk, kcat_s[kb])
        neg_log_keep = jnp.maximum(z, 0.0) + jnp.log(1.0 + jnp.exp(-jnp.abs(z)))
        if diagonal:
            neg_log_keep = jnp.where(causal, neg_log_keep, 0.0)
        a_heads, new_carry = [], []
        for hd in range(2):
            cols = slice(hd * blk, (hd + 1) * blk)
            nlk = neg_log_keep[:, cols]
            later = _dot(nlk.astype(BF16), cum_mat)
            a = jnp.exp(z[:, cols] - nlk - later - carry[hd])
            if diagonal:
                a = jnp.where(causal[:, cols], a, 0.0)
            a_heads.append(a.astype(BF16))
            new_carry.append(carry[hd] + (later[:, 0:1] + nlk[:, 0:1]))
        oacc = oacc + _dot(jnp.concatenate(a_heads, axis=1), vcat_s[kb])
        return new_carry, oacc

    rows = lambda qb: pl.ds(pl.multiple_of(qb * blk, blk), blk)
    zero_carry = [jnp.zeros((blk, 1), F32)] * 2
    zero_acc = jnp.zeros((blk, LANES), F32)

    def head_blocks(qb, slot):
        q_blk = qn_s[rows(qb), :]
        carry, oacc = step(q_blk, qb, zero_carry, zero_acc, True)
        carry, oacc = step(q_blk, qb - 1, carry, oacc, False)
        carry_s[slot, 0], carry_s[slot, 1] = carry
        oacc_s[slot] = oacc
        return q_blk, jnp.min(jnp.minimum(carry[0], carry[1]))

    def tail_blocks(qb, slot, q_blk, low):
        def kv_cond(st):
            kb, low = st
            return jnp.logical_and(kb >= 0, low < -LOG_ZERO_F32)

        def kv_step(st):
            kb, _ = st
            carry, oacc = step(q_blk, kb, [carry_s[slot, 0], carry_s[slot, 1]], oacc_s[slot], False)
            carry_s[slot, 0], carry_s[slot, 1] = carry
            oacc_s[slot] = oacc
            return kb - 1, jnp.min(jnp.minimum(carry[0], carry[1]))

        lax.while_loop(kv_cond, kv_step, (qb - 2, low))
        o_ref[0, rows(qb), :] = oacc_s[slot].astype(o_ref.dtype)

    _, oacc0 = step(qn_s[rows(0), :], 0, zero_carry, zero_acc, True)
    o_ref[0, rows(0), :] = oacc0.astype(o_ref.dtype)

    def q_pair(i, _):
        qa = 1 + 2 * i
        blk_a, low_a = head_blocks(qa, 0)
        blk_b, low_b = head_blocks(qa + 1, 1)
        tail_blocks(qa, 0, blk_a, low_a)
        tail_blocks(qa + 1, 1, blk_b, low_b)
        return 0

    n_pair_trips = (n_blk - 1) // 2
    lax.fori_loop(0, n_pair_trips, q_pair, 0)
    if (n_blk - 1) % 2:
        last = n_blk - 1
        blk_l, low_l = head_blocks(last, 0)
        tail_blocks(last, 0, blk_l, low_l)


def _sb_attention(qkv, q_gain, k_gain, name):
    b, s, d3 = qkv.shape
    d = d3 // 3
    n_pair = d // LANES
    blk = SB_BLOCK
    n_blk = s // blk
    gains = [jnp.tile(g.astype(F32), LANES // SB_HEAD_DIM).reshape(1, LANES) for g in (q_gain, k_gain)]
    col_block = lambda off: pl.BlockSpec((1, s, LANES), lambda bi, hp: (bi, 0, off + hp))
    return pl.pallas_call(
        functools.partial(_sb_attn_kernel, blk=blk),
        grid=(b, n_pair),
        in_specs=[col_block(0), col_block(n_pair), col_block(2 * n_pair),
                  _resident((1, LANES)), _resident((1, LANES))],
        out_specs=pl.BlockSpec((1, s, LANES), lambda bi, hp: (bi, 0, hp)),
        out_shape=jax.ShapeDtypeStruct((b, s, d), BF16),
        scratch_shapes=[
            pltpu.VMEM((s, LANES), BF16),
            pltpu.VMEM((n_blk, 2 * blk, LANES), BF16),
            pltpu.VMEM((n_blk, 2 * blk, LANES), BF16),
            pltpu.VMEM((2, blk, LANES), F32),
            pltpu.VMEM((2, 2, blk, 1), F32),
        ],
        compiler_params=_params(2),
        name=name,
    )(qkv, qkv, qkv, *gains)


def _pool_kernel(h_ref, g_ref, win_ref, wgrp_ref, sc_ref, o_ref, ext_s, *, tm, seq):
    i = pl.program_id(0)
    t0 = (i * tm) % seq

    @pl.when(t0 == 0)
    def _():
        ext_s[0:POOL_HALO, :] = jnp.zeros((POOL_HALO, ext_s.shape[1]), F32)

    x = h_ref[...]
    u = _dot(_rms(x, g_ref[...]).astype(BF16), win_ref[...])
    ext_s[POOL_HALO:POOL_HALO + tm, :] = u

    pos = t0 + lax.broadcasted_iota(jnp.int32, (tm, 1), 0)
    gdim = u.shape[1] // len(POOL_WINDOWS)
    for g, w in enumerate(POOL_WINDOWS):
        cols = slice(g * gdim, (g + 1) * gdim)
        ug = u[:, cols]
        win_sum = ug
        for back in range(1, w):
            win_sum = win_sum + ext_s[POOL_HALO - back:POOL_HALO - back + tm, cols]
        cnt = jnp.minimum(pos + 1, w).astype(F32)
        pooled = win_sum / cnt - ug
        y = _dot(pooled.astype(BF16), wgrp_ref[g]) * sc_ref[:, cols]
        o_ref[:, cols] = x[:, cols] + y

    ext_s[0:POOL_HALO, :] = ext_s[tm:tm + POOL_HALO, :]


def _pool_mixer(h, seq, g, w_in, w_grp, scale, tm, name):
    t, d = h.shape
    ng, gd, _ = w_grp.shape
    return pl.pallas_call(
        functools.partial(_pool_kernel, tm=tm, seq=seq),
        grid=(t // tm,),
        in_specs=[pl.BlockSpec((tm, d), lambda i: (i, 0)), _resident((1, d)), _resident((d, d)),
                  _resident((ng, gd, gd)), _resident((1, d))],
        out_specs=pl.BlockSpec((tm, d), lambda i: (i, 0)),
        out_shape=jax.ShapeDtypeStruct((t, d), F32),
        scratch_shapes=[pltpu.VMEM((POOL_HALO + tm, d), F32)],
        compiler_params=_params(1),
        name=name,
    )(h, g, w_in, w_grp, scale)


_HGRN_LEVELS = (64, 32, 16, 8, 4, 2, 1)


def _hgrn_kernel(q_ref, f_ref, i_ref, g_ref, lbp_ref, og_ref, o_ref, cum_s, level_s, state_s, *,
                 layer_idx):
    c = HGRN_CHUNK
    n_chunk = q_ref.shape[1] // c

    lbp = lbp_ref[...]
    e = jnp.exp(lbp - jnp.max(lbp, axis=0, keepdims=True))
    sm = e / jnp.sum(e, axis=0, keepdims=True)
    lb = jnp.sum(sm[1:layer_idx + 1], axis=0, keepdims=True)
    log_lb = jnp.log(lb)
    log_1m_lb = jnp.log1p(-lb)

    r = lax.broadcasted_iota(jnp.int32, (c, c), 0)
    s = lax.broadcasted_iota(jnp.int32, (c, c), 1)
    one = lambda m: jnp.where(m, 1.0, 0.0).astype(BF16)
    cum_s[0:c, :] = one(s <= r)
    cum_s[c:2 * c, :] = one(s > r)
    for l, m in enumerate(_HGRN_LEVELS):
        sh = m.bit_length() - 1
        same = (r >> sh) == (s >> sh)
        odd = ((r >> sh) & 1) == 1
        cum_s[(2 + l) * c:(3 + l) * c, :] = one(same & (odd == (s <= r)))

    level = jnp.full((c, c), len(_HGRN_LEVELS), jnp.int32)
    for l, m in enumerate(_HGRN_LEVELS):
        sh = m.bit_length() - 1
        level = jnp.where((((r >> sh) & 1) == 1) & ((s >> sh) == (r >> sh) - 1), l, level)
    level_s[...] = level

    state_s[...] = jnp.zeros_like(state_s)

    def chunk(n, _):
        rows = pl.ds(pl.multiple_of(n * c, c), c)
        fg = f_ref[0, rows, :].astype(F32)
        a_ = log_lb
        c_ = log_1m_lb + _log_sigmoid(fg)
        log_f = jnp.maximum(a_, c_) + jnp.log(1.0 + jnp.exp(-jnp.abs(a_ - c_)))
        k = (1.0 - lb) * jax.nn.sigmoid(-fg)
        q = _silu(q_ref[0, rows, :].astype(F32))
        v = i_ref[0, rows, :]

        cum = _dot(cum_s[...], (log_f * LOG2_E).astype(BF16))
        dec = jnp.exp2(cum)
        b_end = dec[c - 1:c, :]

        state = state_s[...]
        o = _dot_nt((q * dec[0:c]).astype(BF16), state.astype(BF16))
        kd = (k * dec[c:2 * c]).astype(BF16)
        v_t = v.astype(F32).T.astype(BF16)
        state_s[...] = state * b_end + _dot(v_t, kd)

        scores = jnp.zeros((c, c), F32)
        level = level_s[...]
        for l in range(len(_HGRN_LEVELS)):
            d_l = dec[(2 + l) * c:(3 + l) * c]
            sc = _dot_nt((q * d_l).astype(BF16), (k * d_l).astype(BF16))
            scores = jnp.where(level == l, sc, scores)
        o = o + _dot(scores.astype(BF16), v)
        o = o + jnp.sum(q * k, axis=-1, keepdims=True) * v.astype(F32)

        o = _rms(o, og_ref[...]) * _silu(g_ref[0, rows, :].astype(F32))
        o_ref[0, rows, :] = o.astype(o_ref.dtype)
        return 0

    lax.fori_loop(0, n_chunk, chunk, 0, unroll=8)


def _hgrn_core(proj, lower_bounds, layer_idx, o_gain, name):
    b, s, d4 = proj.shape
    d = d4 // 4
    n_head = d // LANES
    depth = lower_bounds.shape[0]
    col_block = lambda off: pl.BlockSpec((1, s, LANES), lambda bi, hd: (bi, 0, off + hd))
    return pl.pallas_call(
        functools.partial(_hgrn_kernel, layer_idx=layer_idx),
        grid=(b, n_head),
        in_specs=[col_block(0), col_block(n_head), col_block(2 * n_head), col_block(3 * n_head),
                  pl.BlockSpec((depth, LANES), lambda bi, hd: (0, hd)), _resident((1, LANES))],
        out_specs=pl.BlockSpec((1, s, LANES), lambda bi, hd: (bi, 0, hd)),
        out_shape=jax.ShapeDtypeStruct((b, s, d), BF16),
        scratch_shapes=[pltpu.VMEM(((2 + len(_HGRN_LEVELS)) * HGRN_CHUNK, HGRN_CHUNK), BF16),
                        pltpu.VMEM((HGRN_CHUNK, HGRN_CHUNK), jnp.int32),
                        pltpu.VMEM((LANES, LANES), F32)],
        compiler_params=_params(2),
        name=name,
    )(proj, proj, proj, proj, lower_bounds, o_gain.reshape(1, LANES))


def _swiglu_into(acc_ref, hn, wgu, wd, d_ff, ck, row_scale=None):
    for c0 in range(0, d_ff, ck):
        c1 = min(c0 + ck, d_ff)
        gate = _dot(hn, wgu[:, c0:c1])
        up = _dot(hn, wgu[:, d_ff + c0:d_ff + c1])
        act = _silu(gate) * up
        if row_scale is not None:
            act = act * row_scale
        acc_ref[...] += _dot(act.astype(BF16), wd[c0:c1, :])


def _ple(h, p, gn, wg, wp):
    gate = jax.nn.sigmoid(_dot(_rms(h, gn).astype(BF16), wg))
    return h + gate * _dot(p.astype(BF16), wp)


def _dense_post_kernel(h_ref, y_ref, p_ref, wo_ref, nf_ref, wgu_ref, wd_ref, gn_ref, wg_ref, wp_ref,
                       o_ref, acc_ref, *, ck):
    h1 = h_ref[...] + _dot(y_ref[...], wo_ref[...])
    acc_ref[...] = h1
    hn = _rms(h1, nf_ref[...]).astype(BF16)
    _swiglu_into(acc_ref, hn, wgu_ref, wd_ref, wd_ref.shape[0], ck)
    o_ref[...] = _ple(acc_ref[...], p_ref[...], gn_ref[...], wg_ref[...], wp_ref[...])


def _layer_tile(p_all, layer, tm):
    return pl.BlockSpec((None, tm, p_all.shape[2]), lambda i: (layer, i, 0))


def _dense_post(h, y, p_all, layer, wo, nf, wgu, wd, gn, wg, wp, tm, name):
    t, d = h.shape
    tile = lambda n: pl.BlockSpec((tm, n), lambda i: (i, 0))
    weights = (wo, nf, wgu, wd, gn, wg, wp)
    p = p_all
    return pl.pallas_call(
        functools.partial(_dense_post_kernel, ck=256),
        grid=(t // tm,),
        in_specs=[tile(d), tile(d), _layer_tile(p_all, layer, tm)] + [_resident(w.shape) for w in weights],
        out_specs=tile(d),
        out_shape=jax.ShapeDtypeStruct((t, d), F32),
        scratch_shapes=[pltpu.VMEM((tm, d), F32)],
        compiler_params=_params(1),
        name=name,
    )(h, y, p, *weights)


def _pack_halves(x):
    n = x.shape[1] // 2
    bits = lambda v: lax.bitcast_convert_type(v.astype(BF16).astype(F32), jnp.int32)
    return bits(x[:, n:]) | lax.shift_right_logical(bits(x[:, :n]), 16)


def _unpack_halves(u):
    lo = lax.bitcast_convert_type(lax.shift_left(u, 16), F32)
    hi = lax.bitcast_convert_type(u & jnp.int32(-65536), F32)
    return jnp.concatenate([lo.astype(BF16), hi.astype(BF16)], axis=1)


def _route_tail(h1, nf_ref, wr_ref, cnt_s, hnp_ref, sel_ref, gate_ref, cnt_ref):
    @pl.when(pl.program_id(0) == 0)
    def _():
        cnt_s[...] = jnp.zeros_like(cnt_s)

    hn = _rms(h1, nf_ref[...]).astype(BF16)
    hnp_ref[...] = _pack_halves(hn)
    logits = _dot(hn, wr_ref[...])
    tm, n_exp = logits.shape
    idx = lax.broadcasted_iota(jnp.int32, logits.shape, 1)
    m1 = jnp.max(logits, axis=-1, keepdims=True)
    i1 = jnp.min(jnp.where(logits == m1, idx, n_exp), axis=-1, keepdims=True)
    rest = jnp.where(idx == i1, -jnp.inf, logits)
    m2 = jnp.max(rest, axis=-1, keepdims=True)
    i2 = jnp.min(jnp.where(rest == m2, idx, n_exp), axis=-1, keepdims=True)
    e2 = jnp.exp(m2 - m1)
    g1 = 1.0 / (1.0 + e2)

    onehot = jnp.where(idx == i1, 1.0, jnp.where(idx == i2, 1.0, 0.0))
    r = lax.broadcasted_iota(jnp.int32, (tm, tm), 0)
    c = lax.broadcasted_iota(jnp.int32, (tm, tm), 1)
    before = cnt_s[...] + _dot(jnp.where(c < r, 1.0, 0.0).astype(BF16), onehot.astype(BF16))
    rank1 = jnp.sum(jnp.where(idx == i1, before, 0.0), axis=-1, keepdims=True).astype(jnp.int32)
    rank2 = jnp.sum(jnp.where(idx == i2, before, 0.0), axis=-1, keepdims=True).astype(jnp.int32)
    cnt_s[...] += jnp.sum(onehot, axis=0, keepdims=True)
    cnt_ref[...] = cnt_s[...]

    lane = lax.broadcasted_iota(jnp.int32, sel_ref.shape, 1)
    sel_ref[...] = jnp.where(lane == 0, i1, jnp.where(lane == 1, i2, jnp.where(lane == 2, rank1, rank2)))
    gate_ref[...] = jnp.where(lax.broadcasted_iota(jnp.int32, gate_ref.shape, 1) == 0, g1, e2 * g1)


def _route_proj_kernel(h_ref, y_ref, wo_ref, nf_ref, wr_ref, h1_ref, hnp_ref, sel_ref, gate_ref, cnt_ref,
                       cnt_s):
    h1 = h_ref[...] + _dot(y_ref[...], wo_ref[...])
    h1_ref[...] = h1
    _route_tail(h1, nf_ref, wr_ref, cnt_s, hnp_ref, sel_ref, gate_ref, cnt_ref)


def _route_kernel(h_ref, nf_ref, wr_ref, hnp_ref, sel_ref, gate_ref, cnt_ref, cnt_s):
    _route_tail(h_ref[...], nf_ref, wr_ref, cnt_s, hnp_ref, sel_ref, gate_ref, cnt_ref)


def _route(h, y, wo, nf, wr, tm, name):
    t, d = h.shape
    n_exp = wr.shape[1]
    tile = lambda n: pl.BlockSpec((tm, n), lambda i: (i, 0))
    outs = [jax.ShapeDtypeStruct((t, d // 2), jnp.int32), jax.ShapeDtypeStruct((t, 4), jnp.int32),
            jax.ShapeDtypeStruct((t, TOP_K), F32), jax.ShapeDtypeStruct((1, n_exp), F32)]
    out_specs = [tile(d // 2), tile(4), tile(TOP_K), _resident((1, n_exp))]
    scratch = [pltpu.VMEM((1, n_exp), F32)]
    if y is None:
        res = pl.pallas_call(
            _route_kernel, grid=(t // tm,),
            in_specs=[tile(d), _resident(nf.shape), _resident(wr.shape)],
            out_specs=out_specs, out_shape=outs, scratch_shapes=scratch,
            compiler_params=_params(1), name=name,
        )(h, nf, wr)
        return (h, *res)
    return pl.pallas_call(
        _route_proj_kernel, grid=(t // tm,),
        in_specs=[tile(d), tile(d), _resident(wo.shape), _resident(nf.shape), _resident(wr.shape)],
        out_specs=[tile(d)] + out_specs,
        out_shape=[jax.ShapeDtypeStruct((t, d), F32)] + outs, scratch_shapes=scratch,
        compiler_params=_params(1), name=name,
    )(h, y, wo, nf, wr)


def _sc_mesh():
    return plsc.VectorSubcoreMesh(core_axis_name="c", subcore_axis_name="s")


def _sc_worker():
    return lax.axis_index("s") * SC_CORES + lax.axis_index("c")


def _sc_scatter_rows(x, idx, n_out):
    r, d = x.shape
    per_w = idx.shape[0] // SC_WORKERS
    n_chunk = per_w // SC_CHUNK

    @functools.partial(
        pl.kernel, mesh=_sc_mesh(), out_type=jax.ShapeDtypeStruct((n_out, d), x.dtype),
        scratch_types=[pltpu.VMEM((SC_CHUNK,), jnp.int32), pltpu.VMEM((SC_CHUNK, d), x.dtype),
                       pltpu.SemaphoreType.DMA])
    def scatter(x_hbm, idx_hbm, out_hbm, idx_v, rows_v, sem):
        base = _sc_worker() * per_w

        @pl.loop(0, n_chunk)
        def _(j):
            off = pl.multiple_of(base + j * SC_CHUNK, SC_CHUNK)
            src = pl.multiple_of(off % r, SC_CHUNK)
            pltpu.sync_copy(idx_hbm.at[pl.ds(off, SC_CHUNK)], idx_v)
            pltpu.sync_copy(x_hbm.at[pl.ds(src, SC_CHUNK)], rows_v)
            pltpu.async_copy(rows_v, out_hbm.at[idx_v], sem).wait()

    return scatter(x, idx)


def _sc_gather_rows(table, idx):
    d = table.shape[1]
    b = idx.shape[0]
    per_w = b // SC_WORKERS
    n_chunk = per_w // SC_CHUNK

    @functools.partial(
        pl.kernel, mesh=_sc_mesh(), out_type=jax.ShapeDtypeStruct((b, d), table.dtype),
        scratch_types=[pltpu.VMEM((n_chunk, SC_CHUNK), jnp.int32), pltpu.VMEM((2, SC_CHUNK, d), table.dtype),
                       pltpu.SemaphoreType.DMA((2,))])
    def gather(table_hbm, idx_hbm, out_hbm, idx_v, rows_v, sem):
        wid = _sc_worker()
        pltpu.sync_copy(idx_hbm.at[wid], idx_v)

        def fetch(j, slot):
            return pltpu.make_async_copy(table_hbm.at[idx_v.at[j]], rows_v.at[slot], sem.at[slot])

        fetch(0, 0).start()

        @pl.loop(0, n_chunk, step=2)
        def _(j):
            for slot in range(2):
                jj = j + slot
                fetch(jj, slot).wait()

                @pl.when(jj + 1 < n_chunk)
                def _():
                    fetch(jj + 1, 1 - slot).start()

                off = pl.multiple_of(wid * per_w + jj * SC_CHUNK, SC_CHUNK)
                pltpu.sync_copy(rows_v.at[slot], out_hbm.at[pl.ds(off, SC_CHUNK)])

    return gather(table, idx.reshape(SC_WORKERS, n_chunk, SC_CHUNK))


def _expert_ffn_kernel(te_ref, nu_ref, xs_ref, wgu_ref, wd_ref, ys_ref, acc_ref, wgu_bf, wd_bf, *, ck):
    j = pl.program_id(0)

    @pl.when(j < nu_ref[0])
    def _():
        @pl.when(jnp.logical_or(j == 0, te_ref[j] != te_ref[jnp.maximum(j - 1, 0)]))
        def _():
            for c0 in range(0, wgu_bf.shape[1], ck):
                wgu_bf[:, c0:c0 + ck] = wgu_ref[:, c0:c0 + ck].astype(BF16)
            for c0 in range(0, wd_bf.shape[0], ck):
                c1 = min(c0 + ck, wd_bf.shape[0])
                wd_bf[c0:c1, :] = wd_ref[c0:c1, :].astype(BF16)

        acc_ref[...] = jnp.zeros_like(acc_ref)
        _swiglu_into(acc_ref, _unpack_halves(xs_ref[...]), wgu_bf, wd_bf, wd_bf.shape[0], ck)
        ys_ref[...] = _pack_halves(acc_ref[...])


def _expert_ffn(xs, tile_expert, n_used, wgu_all, wd_all, moe_idx, tr, name):
    n_rows, half = xs.shape
    _, _, d_ffe, d = wd_all.shape
    return pl.pallas_call(
        functools.partial(_expert_ffn_kernel, ck=256),
        grid_spec=pltpu.PrefetchScalarGridSpec(
            num_scalar_prefetch=2,
            grid=(n_rows // tr,),
            in_specs=[pl.BlockSpec((tr, half), lambda j, te, nu: (j, 0)),
                      pl.BlockSpec((None, None, d, 2 * d_ffe), lambda j, te, nu: (moe_idx, te[j], 0, 0)),
                      pl.BlockSpec((None, None, d_ffe, d), lambda j, te, nu: (moe_idx, te[j], 0, 0))],
            out_specs=pl.BlockSpec((tr, half), lambda j, te, nu: (j, 0)),
            scratch_shapes=[pltpu.VMEM((tr, d), F32), pltpu.VMEM((d, 2 * d_ffe), BF16),
                            pltpu.VMEM((d_ffe, d), BF16)]),
        out_shape=jax.ShapeDtypeStruct((n_rows, half), jnp.int32),
        compiler_params=_params(1),
        name=name,
    )(tile_expert, n_used, xs, wgu_all, wd_all)


def _combine_kernel(h1_ref, y1_ref, y2_ref, gate_ref, p_ref, gn_ref, wg_ref, wp_ref, o_ref):
    gate = gate_ref[...]
    h2 = (h1_ref[...] + gate[:, 0:1] * _unpack_halves(y1_ref[...]).astype(F32)
          + gate[:, 1:2] * _unpack_halves(y2_ref[...]).astype(F32))
    o_ref[...] = _ple(h2, p_ref[...], gn_ref[...], wg_ref[...], wp_ref[...])


def _combine(h1, yg, gate, p_all, layer, gn, wg, wp, tm, name):
    t, d = h1.shape
    n_tile = t // tm
    tile = lambda n: pl.BlockSpec((tm, n), lambda i: (i, 0))
    p = p_all
    return pl.pallas_call(
        _combine_kernel, grid=(n_tile,),
        in_specs=[tile(d), tile(d // 2), pl.BlockSpec((tm, d // 2), lambda i: (i + n_tile, 0)),
                  tile(TOP_K), _layer_tile(p_all, layer, tm),
                  _resident(gn.shape), _resident(wg.shape), _resident(wp.shape)],
        out_specs=tile(d),
        out_shape=jax.ShapeDtypeStruct((t, d), F32),
        compiler_params=_params(1),
        name=name,
    )(h1, yg, yg, gate, p, gn, wg, wp)


def _moe_post(h1, hnp, sel, gate, counts, p_all, wgu_all, wd_all, moe_idx, gn, wg, wp, tm, layer):
    t = h1.shape[0]
    n_exp = wd_all.shape[1]
    tr = EXPERT_ROW_TILE
    n_rows = TOP_K * t + n_exp * tr
    cnt = counts[0].astype(jnp.int32)
    padded = (cnt + tr - 1) // tr * tr
    ends = jnp.cumsum(padded)
    starts = ends - padded
    e12, rank = sel[:, :TOP_K], sel[:, TOP_K:]
    pos = jnp.sum(jnp.where(e12[:, :, None] == jnp.arange(n_exp), starts, 0), axis=-1) + rank
    pos = pos.T.reshape(-1)
    n_used = (ends[-1] // tr).reshape(1)
    tile_start = jnp.arange(n_rows // tr, dtype=jnp.int32) * tr
    tile_expert = jnp.minimum(jnp.sum(tile_start[:, None] >= ends[None, :], axis=-1), n_exp - 1)

    xs = _sc_scatter_rows(hnp, pos, n_rows)
    ys = _expert_ffn(xs, tile_expert.astype(jnp.int32), n_used.astype(jnp.int32), wgu_all, wd_all, moe_idx, tr,
                     f"experts_{layer}")
    yg = _sc_gather_rows(ys, pos)
    return _combine(h1, yg, gate, p_all, layer, gn, wg, wp, tm, f"combine_{layer}")


def kernel(x, p, norm_mix, norm_ffn, sb_w_qkv, sb_q_gain, sb_k_gain, sb_w_o, pool_w_in, pool_w_grp,
           pool_scale, hgrn_w_in, hgrn_lower_bounds, hgrn_o_gain, hgrn_w_o, ffn_w_gu, ffn_w_d,
           moe_router, moe_w_gu, moe_w_d, ple_w, ple_gate_norm, ple_gate_w):
    b, s, d = x.shape
    depth = p.shape[0]
    t = b * s
    tm = 512
    bf = lambda w: w.astype(BF16)
    vec = lambda g: g.reshape(1, -1)

    h = x.reshape(t, d)
    p_all = p.reshape(depth, t, -1)
    for i in range(depth):
        kind, j = i % N_MIXERS, i // N_MIXERS
        ple = (vec(ple_gate_norm[i]), bf(ple_gate_w[i]), bf(ple_w[i]))
        y = w_o = None
        if kind == 0:
            qkv = _norm_matmul(h, vec(norm_mix[i]), bf(sb_w_qkv[j]), BF16, tm, f"sb_qkv_{i}")
            y = _sb_attention(qkv.reshape(b, s, 3 * d), sb_q_gain[j], sb_k_gain[j], f"sb_attn_{i}")
            y, w_o = y.reshape(t, d), bf(sb_w_o[j])
        elif kind == 1:
            h = _pool_mixer(h, s, vec(norm_mix[i]), bf(pool_w_in[j]), bf(pool_w_grp[j]),
                            vec(pool_scale[j]), tm, f"pool_{i}")
        else:
            proj = _norm_matmul(h, vec(norm_mix[i]), bf(hgrn_w_in[j]), BF16, tm, f"hgrn_in_{i}")
            y = _hgrn_core(proj.reshape(b, s, 4 * d), hgrn_lower_bounds, i, hgrn_o_gain[j], f"hgrn_{i}")
            y, w_o = y.reshape(t, d), bf(hgrn_w_o[j])

        c = i // 2
        if i % 2 == 0:
            h = _dense_post(h, y, p_all, i, w_o, vec(norm_ffn[i]), bf(ffn_w_gu[c]), bf(ffn_w_d[c]), *ple,
                            tm, f"dense_{i}")
        else:
            routed = _route(h, y, w_o, vec(norm_ffn[i]), bf(moe_router[c]), tm, f"route_{i}")
            h = _moe_post(*routed, p_all, moe_w_gu, moe_w_d, c, *ple, tm, i)
    return h.reshape(b, s, d)
```

```python
import functools

import jax
import jax.numpy as jnp
from jax import lax
from jax.experimental import pallas as pl
from jax.experimental.pallas import tpu as pltpu
from jax.experimental.pallas import tpu_sc as plsc

F32 = jnp.float32
BF16 = jnp.bfloat16

EPS = 1e-6
LOG2_E = 1.4426950408889634
N_MIXERS = 3
SB_HEAD_DIM = 64
POOL_WINDOWS = (2, 4, 8, 16)
HGRN_CHUNK = 128
TOP_K = 2

LANES = 128
SB_BLOCK = 256
POOL_HALO = 16
EXPERT_ROW_TILE = 512
SC_CORES = 2
SC_WORKERS = SC_CORES * 16
SC_CHUNK = 64
VMEM_LIMIT = 56 * 1024 * 1024
LOG_ZERO_F32 = -104.0


def _dot(a, b):
    return jnp.dot(a, b, preferred_element_type=F32)


def _dot_nt(a, b):
    return lax.dot_general(a, b, (((1,), (1,)), ((), ())), preferred_element_type=F32)


def _split_bf16(x):
    hi = x.astype(BF16)
    lo = (x - hi.astype(F32)).astype(BF16)
    return hi, lo


def _rms(x, g):
    ms = jnp.mean(x * x, axis=-1, keepdims=True)
    return x * lax.rsqrt(ms + EPS) * g


def _log_sigmoid(x):
    return -(jnp.maximum(-x, 0.0) + jnp.log(1.0 + jnp.exp(-jnp.abs(x))))


def _silu(x):
    return x * jax.nn.sigmoid(x)


def _params(n_axes):
    return pltpu.CompilerParams(
        dimension_semantics=("arbitrary",) * n_axes, vmem_limit_bytes=VMEM_LIMIT)


def _resident(shape):
    nd = len(shape)
    return pl.BlockSpec(shape, lambda *_: (0,) * nd)


def _norm_matmul_kernel(h_ref, g_ref, w_ref, o_ref, *, n_chunk):
    xn = _rms(h_ref[...], g_ref[...]).astype(BF16)
    for c0 in range(0, w_ref.shape[1], n_chunk):
        o_ref[:, c0:c0 + n_chunk] = _dot(xn, w_ref[:, c0:c0 + n_chunk]).astype(o_ref.dtype)


def _norm_matmul(h, g, w, out_dtype, tm, name):
    t, d = h.shape
    n = w.shape[1]
    return pl.pallas_call(
        functools.partial(_norm_matmul_kernel, n_chunk=512),
        grid=(t // tm,),
        in_specs=[pl.BlockSpec((tm, d), lambda i: (i, 0)), _resident((1, d)), _resident((d, n))],
        out_specs=pl.BlockSpec((tm, n), lambda i: (i, 0)),
        out_shape=jax.ShapeDtypeStruct((t, n), out_dtype),
        compiler_params=_params(1),
        name=name,
    )(h, g, w)


def _sb_attn_kernel(q_ref, k_ref, v_ref, qg_ref, kg_ref, o_ref,
                    qn_s, kcat_s, vcat_s, oacc_s, carry_s, *, blk):
    s_len = q_ref.shape[1]
    n_blk = s_len // blk
    scale = SB_HEAD_DIM ** -0.5

    row = lax.broadcasted_iota(jnp.int32, (LANES, LANES), 0)
    col = lax.broadcasted_iota(jnp.int32, (LANES, LANES), 1)
    head_sum = jnp.where((row >> 6) == (col >> 6), 1.0, 0.0).astype(BF16)
    head0 = lax.broadcasted_iota(jnp.int32, (blk, LANES), 1) < SB_HEAD_DIM

    def head_norm(x, gain):
        hi, lo = _split_bf16(x * x)
        ss = _dot(hi, head_sum) + _dot(lo, head_sum)
        return x * lax.rsqrt(ss * (1.0 / SB_HEAD_DIM) + EPS) * gain

    def prep(c, _):
        r = pl.ds(pl.multiple_of(c * blk, blk), blk)
        qn = head_norm(q_ref[0, r, :].astype(F32), qg_ref[...] * (scale * LOG2_E))
        qn_s[r, :] = qn.astype(BF16)
        kn = head_norm(k_ref[0, r, :].astype(F32), kg_ref[...]).astype(BF16)
        vv = v_ref[0, r, :]
        zero = jnp.zeros_like(kn)
        kcat_s[c] = jnp.concatenate([jnp.where(head0, kn, zero), jnp.where(head0, zero, kn)], axis=0)
        vcat_s[c] = jnp.concatenate([jnp.where(head0, vv, zero), jnp.where(head0, zero, vv)], axis=0)
        return 0

    lax.fori_loop(0, n_blk, prep, 0, unroll=2)

    r2 = lax.broadcasted_iota(jnp.int32, (blk, blk), 0)
    c2 = lax.broadcasted_iota(jnp.int32, (blk, blk), 1)
    cum_mat = jnp.where(r2 > c2, 1.0, 0.0).astype(BF16)
    causal = (lax.broadcasted_iota(jnp.int32, (blk, 2 * blk), 1) & (blk - 1)) < \
        lax.broadcasted_iota(jnp.int32, (blk, 2 * blk), 0)

    def step(q_blk, kb, carry, oacc, diagonal):
        z = _dot_nt(q_blk, kcat_s[kb])
        neg_log_keep = jnp.maximum(z, 0.0) + jnp.log2(1.0 + jnp.exp2(-jnp.abs(z)))
        if diagonal:
            neg_log_keep = jnp.where(causal, neg_log_keep, 0.0)
        a_heads, new_carry = [], []
        for hd in range(2):
            cols = slice(hd * blk, (hd + 1) * blk)
            nlk = neg_log_keep[:, cols]
            later = _dot(nlk.astype(BF16), cum_mat)
            a = jnp.exp2(z[:, cols] - nlk - later - carry[hd])
            if diagonal:
                a = jnp.where(causal[:, cols], a, 0.0)
            a_heads.append(a.astype(BF16))
            new_carry.append(carry[hd] + (later[:, 0:1] + nlk[:, 0:1]))
        oacc = oacc + _dot(jnp.concatenate(a_heads, axis=1), vcat_s[kb])
        return new_carry, oacc

    rows = lambda qb: pl.ds(pl.multiple_of(qb * blk, blk), blk)
    zero_carry = [jnp.zeros((blk, 1), F32)] * 2
    zero_acc = jnp.zeros((blk, LANES), F32)

    def head_blocks(qb, slot):
        q_blk = qn_s[rows(qb), :]
        carry, oacc = step(q_blk, qb, zero_carry, zero_acc, True)
        carry, oacc = step(q_blk, qb - 1, carry, oacc, False)
        carry_s[slot, 0], carry_s[slot, 1] = carry
        oacc_s[slot] = oacc
        return q_blk, jnp.min(jnp.minimum(carry[0], carry[1]))

    def tail_blocks(qb, slot, q_blk, low):
        def kv_cond(st):
            kb, low = st
            return jnp.logical_and(kb >= 0, low < -LOG_ZERO_F32 * LOG2_E)

        def kv_step(st):
            kb, _ = st
            carry, oacc = step(q_blk, kb, [carry_s[slot, 0], carry_s[slot, 1]], oacc_s[slot], False)
            carry_s[slot, 0], carry_s[slot, 1] = carry
            oacc_s[slot] = oacc
            return kb - 1, jnp.min(jnp.minimum(carry[0], carry[1]))

        lax.while_loop(kv_cond, kv_step, (qb - 2, low))
        o_ref[0, rows(qb), :] = oacc_s[slot].astype(o_ref.dtype)

    _, oacc0 = step(qn_s[rows(0), :], 0, zero_carry, zero_acc, True)
    o_ref[0, rows(0), :] = oacc0.astype(o_ref.dtype)

    def q_pair(i, _):
        qa = 1 + 2 * i
        blk_a, low_a = head_blocks(qa, 0)
        blk_b, low_b = head_blocks(qa + 1, 1)
        tail_blocks(qa, 0, blk_a, low_a)
        tail_blocks(qa + 1, 1, blk_b, low_b)
        return 0

    n_pair_trips = (n_blk - 1) // 2
    lax.fori_loop(0, n_pair_trips, q_pair, 0)
    if (n_blk - 1) % 2:
        last = n_blk - 1
        blk_l, low_l = head_blocks(last, 0)
        tail_blocks(last, 0, blk_l, low_l)


def _sb_attention(qkv, q_gain, k_gain, name):
    b, s, d3 = qkv.shape
    d = d3 // 3
    n_pair = d // LANES
    blk = SB_BLOCK
    n_blk = s // blk
    gains = [jnp.tile(g.astype(F32), LANES // SB_HEAD_DIM).reshape(1, LANES) for g in (q_gain, k_gain)]
    col_block = lambda off: pl.BlockSpec((1, s, LANES), lambda bi, hp: (bi, 0, off + hp))
    return pl.pallas_call(
        functools.partial(_sb_attn_kernel, blk=blk),
        grid=(b, n_pair),
        in_specs=[col_block(0), col_block(n_pair), col_block(2 * n_pair),
                  _resident((1, LANES)), _resident((1, LANES))],
        out_specs=pl.BlockSpec((1, s, LANES), lambda bi, hp: (bi, 0, hp)),
        out_shape=jax.ShapeDtypeStruct((b, s, d), BF16),
        scratch_shapes=[
            pltpu.VMEM((s, LANES), BF16),
            pltpu.VMEM((n_blk, 2 * blk, LANES), BF16),
            pltpu.VMEM((n_blk, 2 * blk, LANES), BF16),
            pltpu.VMEM((2, blk, LANES), F32),
            pltpu.VMEM((2, 2, blk, 1), F32),
        ],
        compiler_params=_params(2),
        name=name,
    )(qkv, qkv, qkv, *gains)


def _pool_kernel(h_ref, g_ref, win_ref, wgrp_ref, sc_ref, o_ref, ext_s, *, tm, seq):
    i = pl.program_id(0)
    t0 = (i * tm) % seq

    @pl.when(t0 == 0)
    def _():
        ext_s[0:POOL_HALO, :] = jnp.zeros((POOL_HALO, ext_s.shape[1]), F32)

    x = h_ref[...]
    u = _dot(_rms(x, g_ref[...]).astype(BF16), win_ref[...])
    ext_s[POOL_HALO:POOL_HALO + tm, :] = u

    pos = t0 + lax.broadcasted_iota(jnp.int32, (tm, 1), 0)
    gdim = u.shape[1] // len(POOL_WINDOWS)
    for g, w in enumerate(POOL_WINDOWS):
        cols = slice(g * gdim, (g + 1) * gdim)
        ug = u[:, cols]
        run = ext_s[:, cols]
        span = 1
        while span < w:
            run = run + pltpu.roll(run, span, axis=0)
            span *= 2
        win_sum = run[POOL_HALO:POOL_HALO + tm, :]
        cnt = jnp.minimum(pos + 1, w).astype(F32)
        pooled = win_sum / cnt - ug
        y = _dot(pooled.astype(BF16), wgrp_ref[g]) * sc_ref[:, cols]
        o_ref[:, cols] = x[:, cols] + y

    ext_s[0:POOL_HALO, :] = ext_s[tm:tm + POOL_HALO, :]


def _pool_mixer(h, seq, g, w_in, w_grp, scale, tm, name):
    t, d = h.shape
    ng, gd, _ = w_grp.shape
    return pl.pallas_call(
        functools.partial(_pool_kernel, tm=tm, seq=seq),
        grid=(t // tm,),
        in_specs=[pl.BlockSpec((tm, d), lambda i: (i, 0)), _resident((1, d)), _resident((d, d)),
                  _resident((ng, gd, gd)), _resident((1, d))],
        out_specs=pl.BlockSpec((tm, d), lambda i: (i, 0)),
        out_shape=jax.ShapeDtypeStruct((t, d), F32),
        scratch_shapes=[pltpu.VMEM((POOL_HALO + tm, d), F32)],
        compiler_params=_params(1),
        name=name,
    )(h, g, w_in, w_grp, scale)


_HGRN_LEVELS = (64, 32, 16, 8, 4, 2, 1)


def _hgrn_kernel(q_ref, f_ref, i_ref, g_ref, lbp_ref, og_ref, o_ref, cum_s, level_s, state_s, *,
                 layer_idx):
    c = HGRN_CHUNK
    n_chunk = q_ref.shape[1] // c

    lbp = lbp_ref[...]
    e = jnp.exp(lbp - jnp.max(lbp, axis=0, keepdims=True))
    sm = e / jnp.sum(e, axis=0, keepdims=True)
    lb = jnp.sum(sm[1:layer_idx + 1], axis=0, keepdims=True)
    log_lb = jnp.log(lb)
    log_1m_lb = jnp.log1p(-lb)

    r = lax.broadcasted_iota(jnp.int32, (c, c), 0)
    s = lax.broadcasted_iota(jnp.int32, (c, c), 1)
    one = lambda m: jnp.where(m, 1.0, 0.0).astype(BF16)
    cum_s[0:c, :] = one(s <= r)
    cum_s[c:2 * c, :] = one(s > r)
    for l, m in enumerate(_HGRN_LEVELS):
        sh = m.bit_length() - 1
        same = (r >> sh) == (s >> sh)
        odd = ((r >> sh) & 1) == 1
        cum_s[(2 + l) * c:(3 + l) * c, :] = one(same & (odd == (s <= r)))

    level = jnp.full((c, c), len(_HGRN_LEVELS), jnp.int32)
    for l, m in enumerate(_HGRN_LEVELS):
        sh = m.bit_length() - 1
        level = jnp.where((((r >> sh) & 1) == 1) & ((s >> sh) == (r >> sh) - 1), l, level)
    level_s[...] = level

    state_s[...] = jnp.zeros_like(state_s)

    def chunk(n, _):
        rows = pl.ds(pl.multiple_of(n * c, c), c)
        fg = f_ref[0, rows, :].astype(F32)
        a_ = log_lb
        c_ = log_1m_lb + _log_sigmoid(fg)
        log_f = jnp.maximum(a_, c_) + jnp.log(1.0 + jnp.exp(-jnp.abs(a_ - c_)))
        k = (1.0 - lb) * jax.nn.sigmoid(-fg)
        q = _silu(q_ref[0, rows, :].astype(F32))
        v = i_ref[0, rows, :]

        cum = _dot(cum_s[...], (log_f * LOG2_E).astype(BF16))
        dec = jnp.exp2(cum)
        b_end = dec[c - 1:c, :]

        state = state_s[...]
        o = _dot_nt((q * dec[0:c]).astype(BF16), state.astype(BF16))
        kd = (k * dec[c:2 * c]).astype(BF16)
        v_t = v.astype(F32).T.astype(BF16)
        state_s[...] = state * b_end + _dot(v_t, kd)

        scores = jnp.zeros((c, c), F32)
        level = level_s[...]
        for l in range(len(_HGRN_LEVELS)):
            d_l = dec[(2 + l) * c:(3 + l) * c]
            sc = _dot_nt((q * d_l).astype(BF16), (k * d_l).astype(BF16))
            scores = jnp.where(level == l, sc, scores)
        o = o + _dot(scores.astype(BF16), v)
        o = o + jnp.sum(q * k, axis=-1, keepdims=True) * v.astype(F32)

        o = _rms(o, og_ref[...]) * _silu(g_ref[0, rows, :].astype(F32))
        o_ref[0, rows, :] = o.astype(o_ref.dtype)
        return 0

    lax.fori_loop(0, n_chunk, chunk, 0, unroll=8)


def _hgrn_core(proj, lower_bounds, layer_idx, o_gain, name):
    b, s, d4 = proj.shape
    d = d4 // 4
    n_head = d // LANES
    depth = lower_bounds.shape[0]
    col_block = lambda off: pl.BlockSpec((1, s, LANES), lambda bi, hd: (bi, 0, off + hd))
    return pl.pallas_call(
        functools.partial(_hgrn_kernel, layer_idx=layer_idx),
        grid=(b, n_head),
        in_specs=[col_block(0), col_block(n_head), col_block(2 * n_head), col_block(3 * n_head),
                  pl.BlockSpec((depth, LANES), lambda bi, hd: (0, hd)), _resident((1, LANES))],
        out_specs=pl.BlockSpec((1, s, LANES), lambda bi, hd: (bi, 0, hd)),
        out_shape=jax.ShapeDtypeStruct((b, s, d), BF16),
        scratch_shapes=[pltpu.VMEM(((2 + len(_HGRN_LEVELS)) * HGRN_CHUNK, HGRN_CHUNK), BF16),
                        pltpu.VMEM((HGRN_CHUNK, HGRN_CHUNK), jnp.int32),
                        pltpu.VMEM((LANES, LANES), F32)],
        compiler_params=_params(2),
        name=name,
    )(proj, proj, proj, proj, lower_bounds, o_gain.reshape(1, LANES))


def _swiglu_into(acc_ref, hn, wgu, wd, d_ff, ck, row_scale=None):
    for c0 in range(0, d_ff, ck):
        c1 = min(c0 + ck, d_ff)
        gate = _dot(hn, wgu[:, c0:c1])
        up = _dot(hn, wgu[:, d_ff + c0:d_ff + c1])
        act = _silu(gate) * up
        if row_scale is not None:
            act = act * row_scale
        acc_ref[...] += _dot(act.astype(BF16), wd[c0:c1, :])


def _ple(h, p, gn, wg, wp):
    gate = jax.nn.sigmoid(_dot(_rms(h, gn).astype(BF16), wg))
    return h + gate * _dot(p.astype(BF16), wp)


def _dense_post_kernel(h_ref, y_ref, p_ref, wo_ref, nf_ref, wgu_ref, wd_ref, gn_ref, wg_ref, wp_ref,
                       o_ref, acc_ref, *, ck):
    h1 = h_ref[...] + _dot(y_ref[...], wo_ref[...])
    acc_ref[...] = h1
    hn = _rms(h1, nf_ref[...]).astype(BF16)
    _swiglu_into(acc_ref, hn, wgu_ref, wd_ref, wd_ref.shape[0], ck)
    o_ref[...] = _ple(acc_ref[...], p_ref[...], gn_ref[...], wg_ref[...], wp_ref[...])


def _layer_tile(p_all, layer, tm):
    return pl.BlockSpec((None, tm, p_all.shape[2]), lambda i: (layer, i, 0))


def _dense_post(h, y, p_all, layer, wo, nf, wgu, wd, gn, wg, wp, tm, name):
    t, d = h.shape
    tile = lambda n: pl.BlockSpec((tm, n), lambda i: (i, 0))
    weights = (wo, nf, wgu, wd, gn, wg, wp)
    p = p_all
    return pl.pallas_call(
        functools.partial(_dense_post_kernel, ck=256),
        grid=(t // tm,),
        in_specs=[tile(d), tile(d), _layer_tile(p_all, layer, tm)] + [_resident(w.shape) for w in weights],
        out_specs=tile(d),
        out_shape=jax.ShapeDtypeStruct((t, d), F32),
        scratch_shapes=[pltpu.VMEM((tm, d), F32)],
        compiler_params=_params(1),
        name=name,
    )(h, y, p, *weights)


def _pack_halves(x):
    n = x.shape[1] // 2
    bits = lambda v: lax.bitcast_convert_type(v.astype(BF16).astype(F32), jnp.int32)
    return bits(x[:, n:]) | lax.shift_right_logical(bits(x[:, :n]), 16)


def _unpack_halves(u):
    lo = lax.bitcast_convert_type(lax.shift_left(u, 16), F32)
    hi = lax.bitcast_convert_type(u & jnp.int32(-65536), F32)
    return jnp.concatenate([lo.astype(BF16), hi.astype(BF16)], axis=1)


def _route_tail(h1, nf_ref, wr_ref, cnt_s, hnp_ref, sel_ref, gate_ref, cnt_ref):
    @pl.when(pl.program_id(0) == 0)
    def _():
        cnt_s[...] = jnp.zeros_like(cnt_s)

    hn = _rms(h1, nf_ref[...]).astype(BF16)
    hnp_ref[...] = _pack_halves(hn)
    logits = _dot(hn, wr_ref[...])
    tm, n_exp = logits.shape
    idx = lax.broadcasted_iota(jnp.int32, logits.shape, 1)
    m1 = jnp.max(logits, axis=-1, keepdims=True)
    i1 = jnp.min(jnp.where(logits == m1, idx, n_exp), axis=-1, keepdims=True)
    rest = jnp.where(idx == i1, -jnp.inf, logits)
    m2 = jnp.max(rest, axis=-1, keepdims=True)
    i2 = jnp.min(jnp.where(rest == m2, idx, n_exp), axis=-1, keepdims=True)
    e2 = jnp.exp(m2 - m1)
    g1 = 1.0 / (1.0 + e2)

    onehot = jnp.where(idx == i1, 1.0, jnp.where(idx == i2, 1.0, 0.0))
    r = lax.broadcasted_iota(jnp.int32, (tm, tm), 0)
    c = lax.broadcasted_iota(jnp.int32, (tm, tm), 1)
    before = cnt_s[...] + _dot(jnp.where(c < r, 1.0, 0.0).astype(BF16), onehot.astype(BF16))
    rank1 = jnp.sum(jnp.where(idx == i1, before, 0.0), axis=-1, keepdims=True).astype(jnp.int32)
    rank2 = jnp.sum(jnp.where(idx == i2, before, 0.0), axis=-1, keepdims=True).astype(jnp.int32)
    cnt_s[...] += jnp.sum(onehot, axis=0, keepdims=True)
    cnt_ref[...] = cnt_s[...]

    lane = lax.broadcasted_iota(jnp.int32, sel_ref.shape, 1)
    sel_ref[...] = jnp.where(lane == 0, i1, jnp.where(lane == 1, i2, jnp.where(lane == 2, rank1, rank2)))
    gate_ref[...] = jnp.where(lax.broadcasted_iota(jnp.int32, gate_ref.shape, 1) == 0, g1, e2 * g1)


def _route_proj_kernel(h_ref, y_ref, wo_ref, nf_ref, wr_ref, h1_ref, hnp_ref, sel_ref, gate_ref, cnt_ref,
                       cnt_s):
    h1 = h_ref[...] + _dot(y_ref[...], wo_ref[...])
    h1_ref[...] = h1
    _route_tail(h1, nf_ref, wr_ref, cnt_s, hnp_ref, sel_ref, gate_ref, cnt_ref)


def _route_kernel(h_ref, nf_ref, wr_ref, hnp_ref, sel_ref, gate_ref, cnt_ref, cnt_s):
    _route_tail(h_ref[...], nf_ref, wr_ref, cnt_s, hnp_ref, sel_ref, gate_ref, cnt_ref)


def _route(h, y, wo, nf, wr, tm, name):
    t, d = h.shape
    n_exp = wr.shape[1]
    tile = lambda n: pl.BlockSpec((tm, n), lambda i: (i, 0))
    outs = [jax.ShapeDtypeStruct((t, d // 2), jnp.int32), jax.ShapeDtypeStruct((t, 4), jnp.int32),
            jax.ShapeDtypeStruct((t, TOP_K), F32), jax.ShapeDtypeStruct((1, n_exp), F32)]
    out_specs = [tile(d // 2), tile(4), tile(TOP_K), _resident((1, n_exp))]
    scratch = [pltpu.VMEM((1, n_exp), F32)]
    if y is None:
        res = pl.pallas_call(
            _route_kernel, grid=(t // tm,),
            in_specs=[tile(d), _resident(nf.shape), _resident(wr.shape)],
            out_specs=out_specs, out_shape=outs, scratch_shapes=scratch,
            compiler_params=_params(1), name=name,
        )(h, nf, wr)
        return (h, *res)
    return pl.pallas_call(
        _route_proj_kernel, grid=(t // tm,),
        in_specs=[tile(d), tile(d), _resident(wo.shape), _resident(nf.shape), _resident(wr.shape)],
        out_specs=[tile(d)] + out_specs,
        out_shape=[jax.ShapeDtypeStruct((t, d), F32)] + outs, scratch_shapes=scratch,
        compiler_params=_params(1), name=name,
    )(h, y, wo, nf, wr)


def _sc_mesh():
    return plsc.VectorSubcoreMesh(core_axis_name="c", subcore_axis_name="s")


def _sc_worker():
    return lax.axis_index("s") * SC_CORES + lax.axis_index("c")


def _sc_scatter_rows(x, idx, n_out):
    r, d = x.shape
    per_w = idx.shape[0] // SC_WORKERS
    n_chunk = per_w // SC_CHUNK

    @functools.partial(
        pl.kernel, mesh=_sc_mesh(), out_type=jax.ShapeDtypeStruct((n_out, d), x.dtype),
        scratch_types=[pltpu.VMEM((n_chunk, SC_CHUNK), jnp.int32), pltpu.VMEM((2, SC_CHUNK, d), x.dtype),
                       pltpu.SemaphoreType.DMA((2,))])
    def scatter(x_hbm, idx_hbm, out_hbm, idx_v, rows_v, sem):
        wid = _sc_worker()
        pltpu.sync_copy(idx_hbm.at[wid], idx_v)

        def put(j, slot):
            return pltpu.make_async_copy(rows_v.at[slot], out_hbm.at[idx_v.at[j]], sem.at[slot])

        @pl.loop(0, n_chunk, step=2)
        def _(j):
            for slot in range(2):
                jj = j + slot

                @pl.when(jj >= 2)
                def _():
                    put(jj - 2, slot).wait()

                src = pl.multiple_of((wid * per_w + jj * SC_CHUNK) % r, SC_CHUNK)
                pltpu.sync_copy(x_hbm.at[pl.ds(src, SC_CHUNK)], rows_v.at[slot])
                put(jj, slot).start()

        put(n_chunk - 2, 0).wait()
        put(n_chunk - 1, 1).wait()

    return scatter(x, idx.reshape(SC_WORKERS, n_chunk, SC_CHUNK))


def _sc_gather_rows(table, idx):
    d = table.shape[1]
    b = idx.shape[0]
    per_w = b // SC_WORKERS
    n_chunk = per_w // SC_CHUNK

    @functools.partial(
        pl.kernel, mesh=_sc_mesh(), out_type=jax.ShapeDtypeStruct((b, d), table.dtype),
        scratch_types=[pltpu.VMEM((n_chunk, SC_CHUNK), jnp.int32), pltpu.VMEM((2, SC_CHUNK, d), table.dtype),
                       pltpu.SemaphoreType.DMA((2,))])
    def gather(table_hbm, idx_hbm, out_hbm, idx_v, rows_v, sem):
        wid = _sc_worker()
        pltpu.sync_copy(idx_hbm.at[wid], idx_v)

        def fetch(j, slot):
            return pltpu.make_async_copy(table_hbm.at[idx_v.at[j]], rows_v.at[slot], sem.at[slot])

        fetch(0, 0).start()

        @pl.loop(0, n_chunk, step=2)
        def _(j):
            for slot in range(2):
                jj = j + slot
                fetch(jj, slot).wait()

                @pl.when(jj + 1 < n_chunk)
                def _():
                    fetch(jj + 1, 1 - slot).start()

                off = pl.multiple_of(wid * per_w + jj * SC_CHUNK, SC_CHUNK)
                pltpu.sync_copy(rows_v.at[slot], out_hbm.at[pl.ds(off, SC_CHUNK)])

    return gather(table, idx.reshape(SC_WORKERS, n_chunk, SC_CHUNK))


def _expert_ffn_kernel(te_ref, nu_ref, xs_ref, wgu_ref, wd_ref, ys_ref, acc_ref, wgu_bf, wd_bf, *, ck):
    j = pl.program_id(0)

    @pl.when(j < nu_ref[0])
    def _():
        @pl.when(jnp.logical_or(j == 0, te_ref[j] != te_ref[jnp.maximum(j - 1, 0)]))
        def _():
            for c0 in range(0, wgu_bf.shape[1], ck):
                wgu_bf[:, c0:c0 + ck] = wgu_ref[:, c0:c0 + ck].astype(BF16)
            for c0 in range(0, wd_bf.shape[0], ck):
                c1 = min(c0 + ck, wd_bf.shape[0])
                wd_bf[c0:c1, :] = wd_ref[c0:c1, :].astype(BF16)

        acc_ref[...] = jnp.zeros_like(acc_ref)
        _swiglu_into(acc_ref, _unpack_halves(xs_ref[...]), wgu_bf, wd_bf, wd_bf.shape[0], ck)
        ys_ref[...] = _pack_halves(acc_ref[...])


def _expert_ffn(xs, tile_expert, n_used, wgu_all, wd_all, moe_idx, tr, name):
    n_rows, half = xs.shape
    _, _, d_ffe, d = wd_all.shape
    return pl.pallas_call(
        functools.partial(_expert_ffn_kernel, ck=256),
        grid_spec=pltpu.PrefetchScalarGridSpec(
            num_scalar_prefetch=2,
            grid=(n_rows // tr,),
            in_specs=[pl.BlockSpec((tr, half), lambda j, te, nu: (j, 0)),
                      pl.BlockSpec((None, None, d, 2 * d_ffe), lambda j, te, nu: (moe_idx, te[j], 0, 0)),
                      pl.BlockSpec((None, None, d_ffe, d), lambda j, te, nu: (moe_idx, te[j], 0, 0))],
            out_specs=pl.BlockSpec((tr, half), lambda j, te, nu: (j, 0)),
            scratch_shapes=[pltpu.VMEM((tr, d), F32), pltpu.VMEM((d, 2 * d_ffe), BF16),
                            pltpu.VMEM((d_ffe, d), BF16)]),
        out_shape=jax.ShapeDtypeStruct((n_rows, half), jnp.int32),
        compiler_params=_params(1),
        name=name,
    )(tile_expert, n_used, xs, wgu_all, wd_all)


def _combine_kernel(h1_ref, y1_ref, y2_ref, gate_ref, p_ref, gn_ref, wg_ref, wp_ref, o_ref):
    gate = gate_ref[...]
    h2 = (h1_ref[...] + gate[:, 0:1] * _unpack_halves(y1_ref[...]).astype(F32)
          + gate[:, 1:2] * _unpack_halves(y2_ref[...]).astype(F32))
    o_ref[...] = _ple(h2, p_ref[...], gn_ref[...], wg_ref[...], wp_ref[...])


def _combine(h1, yg, gate, p_all, layer, gn, wg, wp, tm, name):
    t, d = h1.shape
    n_tile = t // tm
    tile = lambda n: pl.BlockSpec((tm, n), lambda i: (i, 0))
    p = p_all
    return pl.pallas_call(
        _combine_kernel, grid=(n_tile,),
        in_specs=[tile(d), tile(d // 2), pl.BlockSpec((tm, d // 2), lambda i: (i + n_tile, 0)),
                  tile(TOP_K), _layer_tile(p_all, layer, tm),
                  _resident(gn.shape), _resident(wg.shape), _resident(wp.shape)],
        out_specs=tile(d),
        out_shape=jax.ShapeDtypeStruct((t, d), F32),
        compiler_params=_params(1),
        name=name,
    )(h1, yg, yg, gate, p, gn, wg, wp)


def _moe_post(h1, hnp, sel, gate, counts, p_all, wgu_all, wd_all, moe_idx, gn, wg, wp, tm, layer):
    t = h1.shape[0]
    n_exp = wd_all.shape[1]
    tr = EXPERT_ROW_TILE
    n_rows = TOP_K * t + n_exp * tr
    cnt = counts[0].astype(jnp.int32)
    padded = (cnt + tr - 1) // tr * tr
    ends = jnp.cumsum(padded)
    starts = ends - padded
    e12, rank = sel[:, :TOP_K], sel[:, TOP_K:]
    pos = jnp.sum(jnp.where(e12[:, :, None] == jnp.arange(n_exp), starts, 0), axis=-1) + rank
    pos = pos.T.reshape(-1)
    n_used = (ends[-1] // tr).reshape(1)
    tile_start = jnp.arange(n_rows // tr, dtype=jnp.int32) * tr
    tile_expert = jnp.minimum(jnp.sum(tile_start[:, None] >= ends[None, :], axis=-1), n_exp - 1)

    xs = _sc_scatter_rows(hnp, pos, n_rows)
    ys = _expert_ffn(xs, tile_expert.astype(jnp.int32), n_used.astype(jnp.int32), wgu_all, wd_all, moe_idx, tr,
                     f"experts_{layer}")
    yg = _sc_gather_rows(ys, pos)
    return _combine(h1, yg, gate, p_all, layer, gn, wg, wp, tm, f"combine_{layer}")


def kernel(x, p, norm_mix, norm_ffn, sb_w_qkv, sb_q_gain, sb_k_gain, sb_w_o, pool_w_in, pool_w_grp,
           pool_scale, hgrn_w_in, hgrn_lower_bounds, hgrn_o_gain, hgrn_w_o, ffn_w_gu, ffn_w_d,
           moe_router, moe_w_gu, moe_w_d, ple_w, ple_gate_norm, ple_gate_w):
    b, s, d = x.shape
    depth = p.shape[0]
    t = b * s
    tm = 512
    bf = lambda w: w.astype(BF16)
    vec = lambda g: g.reshape(1, -1)

    h = x.reshape(t, d)
    p_all = p.reshape(depth, t, -1)
    for i in range(depth):
        kind, j = i % N_MIXERS, i // N_MIXERS
        ple = (vec(ple_gate_norm[i]), bf(ple_gate_w[i]), bf(ple_w[i]))
        y = w_o = None
        if kind == 0:
            qkv = _norm_matmul(h, vec(norm_mix[i]), bf(sb_w_qkv[j]), BF16, tm, f"sb_qkv_{i}")
            y = _sb_attention(qkv.reshape(b, s, 3 * d), sb_q_gain[j], sb_k_gain[j], f"sb_attn_{i}")
            y, w_o = y.reshape(t, d), bf(sb_w_o[j])
        elif kind == 1:
            h = _pool_mixer(h, s, vec(norm_mix[i]), bf(pool_w_in[j]), bf(pool_w_grp[j]),
                            vec(pool_scale[j]), tm, f"pool_{i}")
        else:
            proj = _norm_matmul(h, vec(norm_mix[i]), bf(hgrn_w_in[j]), BF16, tm, f"hgrn_in_{i}")
            y = _hgrn_core(proj.reshape(b, s, 4 * d), hgrn_lower_bounds, i, hgrn_o_gain[j], f"hgrn_{i}")
            y, w_o = y.reshape(t, d), bf(hgrn_w_o[j])

        c = i // 2
        if i % 2 == 0:
            h = _dense_post(h, y, p_all, i, w_o, vec(norm_ffn[i]), bf(ffn_w_gu[c]), bf(ffn_w_d[c]), *ple,
                            tm, f"dense_{i}")
        else:
            routed = _route(h, y, w_o, vec(norm_ffn[i]), bf(moe_router[c]), tm, f"route_{i}")
            h = _moe_post(*routed, p_all, moe_w_gu, moe_w_d, c, *ple, tm, i)
    return h.reshape(b, s, d)
```

```python
import functools

import jax
import jax.numpy as jnp
from jax import lax
from jax.experimental import pallas as pl
from jax.experimental.pallas import tpu as pltpu
from jax.experimental.pallas import tpu_sc as plsc

F32 = jnp.float32
BF16 = jnp.bfloat16

EPS = 1e-6
LOG2_E = 1.4426950408889634
N_MIXERS = 3
SB_HEAD_DIM = 64
POOL_WINDOWS = (2, 4, 8, 16)
HGRN_CHUNK = 128
TOP_K = 2

LANES = 128
SB_BLOCK = 256
POOL_HALO = 16
EXPERT_ROW_TILE = 512
SC_CORES = 2
SC_WORKERS = SC_CORES * 16
SC_CHUNK = 64
VMEM_LIMIT = 56 * 1024 * 1024
LOG_ZERO_F32 = -104.0


def _dot(a, b):
    return jnp.dot(a, b, preferred_element_type=F32)


def _dot_nt(a, b):
    return lax.dot_general(a, b, (((1,), (1,)), ((), ())), preferred_element_type=F32)


def _split_bf16(x):
    hi = x.astype(BF16)
    lo = (x - hi.astype(F32)).astype(BF16)
    return hi, lo


def _rms(x, g):
    ms = jnp.mean(x * x, axis=-1, keepdims=True)
    return x * lax.rsqrt(ms + EPS) * g


def _log_sigmoid(x):
    return -(jnp.maximum(-x, 0.0) + jnp.log(1.0 + jnp.exp(-jnp.abs(x))))


def _silu(x):
    return x * jax.nn.sigmoid(x)


def _params(n_axes):
    return pltpu.CompilerParams(
        dimension_semantics=("arbitrary",) * n_axes, vmem_limit_bytes=VMEM_LIMIT)


def _resident(shape):
    nd = len(shape)
    return pl.BlockSpec(shape, lambda *_: (0,) * nd)


def _norm_matmul_kernel(h_ref, g_ref, w_ref, o_ref, *, n_chunk):
    xn = _rms(h_ref[...], g_ref[...]).astype(BF16)
    for c0 in range(0, w_ref.shape[1], n_chunk):
        o_ref[:, c0:c0 + n_chunk] = _dot(xn, w_ref[:, c0:c0 + n_chunk]).astype(o_ref.dtype)


def _norm_matmul(h, g, w, out_dtype, tm, name):
    t, d = h.shape
    n = w.shape[1]
    return pl.pallas_call(
        functools.partial(_norm_matmul_kernel, n_chunk=512),
        grid=(t // tm,),
        in_specs=[pl.BlockSpec((tm, d), lambda i: (i, 0)), _resident((1, d)), _resident((d, n))],
        out_specs=pl.BlockSpec((tm, n), lambda i: (i, 0)),
        out_shape=jax.ShapeDtypeStruct((t, n), out_dtype),
        compiler_params=_params(1),
        name=name,
    )(h, g, w)


def _sb_attn_kernel(q_ref, k_ref, v_ref, qg_ref, kg_ref, o_ref,
                    qn_s, kcat_s, vcat_s, oacc_s, carry_s, *, blk):
    s_len = q_ref.shape[1]
    n_blk = s_len // blk
    scale = SB_HEAD_DIM ** -0.5

    row = lax.broadcasted_iota(jnp.int32, (LANES, LANES), 0)
    col = lax.broadcasted_iota(jnp.int32, (LANES, LANES), 1)
    head_sum = jnp.where((row >> 6) == (col >> 6), 1.0, 0.0).astype(BF16)
    head0 = lax.broadcasted_iota(jnp.int32, (blk, LANES), 1) < SB_HEAD_DIM

    def head_norm(x, gain):
        hi, lo = _split_bf16(x * x)
        ss = _dot(hi, head_sum) + _dot(lo, head_sum)
        return x * lax.rsqrt(ss * (1.0 / SB_HEAD_DIM) + EPS) * gain

    def prep(c, _):
        r = pl.ds(pl.multiple_of(c * blk, blk), blk)
        qn = head_norm(q_ref[0, r, :].astype(F32), qg_ref[...] * (scale * LOG2_E))
        qn_s[r, :] = qn.astype(BF16)
        kn = head_norm(k_ref[0, r, :].astype(F32), kg_ref[...]).astype(BF16)
        vv = v_ref[0, r, :]
        zero = jnp.zeros_like(kn)
        kcat_s[c] = jnp.concatenate([jnp.where(head0, kn, zero), jnp.where(head0, zero, kn)], axis=0)
        vcat_s[c] = jnp.concatenate([jnp.where(head0, vv, zero), jnp.where(head0, zero, vv)], axis=0)
        return 0

    lax.fori_loop(0, n_blk, prep, 0, unroll=4)

    r2 = lax.broadcasted_iota(jnp.int32, (blk, blk), 0)
    c2 = lax.broadcasted_iota(jnp.int32, (blk, blk), 1)
    cum_mat = jnp.where(r2 > c2, 1.0, 0.0).astype(BF16)
    causal = (lax.broadcasted_iota(jnp.int32, (blk, 2 * blk), 1) & (blk - 1)) < \
        lax.broadcasted_iota(jnp.int32, (blk, 2 * blk), 0)

    def step(q_blk, kb, carry, oacc, diagonal):
        z = _dot_nt(q_blk, kcat_s[kb])
        neg_log_keep = jnp.maximum(z, 0.0) + jnp.log2(1.0 + jnp.exp2(-jnp.abs(z)))
        if diagonal:
            neg_log_keep = jnp.where(causal, neg_log_keep, 0.0)
        a_heads, new_carry = [], []
        for hd in range(2):
            cols = slice(hd * blk, (hd + 1) * blk)
            nlk = neg_log_keep[:, cols]
            later = _dot(nlk.astype(BF16), cum_mat)
            a = jnp.exp2(z[:, cols] - nlk - later - carry[hd])
            if diagonal:
                a = jnp.where(causal[:, cols], a, 0.0)
            a_heads.append(a.astype(BF16))
            new_carry.append(carry[hd] + (later[:, 0:1] + nlk[:, 0:1]))
        oacc = oacc + _dot(jnp.concatenate(a_heads, axis=1), vcat_s[kb])
        return new_carry, oacc

    rows = lambda qb: pl.ds(pl.multiple_of(qb * blk, blk), blk)
    zero_carry = [jnp.zeros((blk, 1), F32)] * 2
    zero_acc = jnp.zeros((blk, LANES), F32)

    def head_blocks(qb, slot):
        q_blk = qn_s[rows(qb), :]
        carry, oacc = step(q_blk, qb, zero_carry, zero_acc, True)
        carry, oacc = step(q_blk, qb - 1, carry, oacc, False)
        carry_s[slot, 0], carry_s[slot, 1] = carry
        oacc_s[slot] = oacc
        return q_blk, jnp.min(jnp.minimum(carry[0], carry[1]))

    def tail_blocks(qb, slot, q_blk, low):
        def kv_cond(st):
            kb, low = st
            return jnp.logical_and(kb >= 0, low < -LOG_ZERO_F32 * LOG2_E)

        def kv_step(st):
            kb, _ = st
            carry, oacc = step(q_blk, kb, [carry_s[slot, 0], carry_s[slot, 1]], oacc_s[slot], False)
            carry_s[slot, 0], carry_s[slot, 1] = carry
            oacc_s[slot] = oacc
            return kb - 1, jnp.min(jnp.minimum(carry[0], carry[1]))

        lax.while_loop(kv_cond, kv_step, (qb - 2, low))
        o_ref[0, rows(qb), :] = oacc_s[slot].astype(o_ref.dtype)

    _, oacc0 = step(qn_s[rows(0), :], 0, zero_carry, zero_acc, True)
    o_ref[0, rows(0), :] = oacc0.astype(o_ref.dtype)

    def q_pair(i, _):
        qa = 1 + 2 * i
        blk_a, low_a = head_blocks(qa, 0)
        blk_b, low_b = head_blocks(qa + 1, 1)
        tail_blocks(qa, 0, blk_a, low_a)
        tail_blocks(qa + 1, 1, blk_b, low_b)
        return 0

    n_pair_trips = (n_blk - 1) // 2
    lax.fori_loop(0, n_pair_trips, q_pair, 0)
    if (n_blk - 1) % 2:
        last = n_blk - 1
        blk_l, low_l = head_blocks(last, 0)
        tail_blocks(last, 0, blk_l, low_l)


def _sb_attention(qkv, q_gain, k_gain, name):
    b, s, d3 = qkv.shape
    d = d3 // 3
    n_pair = d // LANES
    blk = SB_BLOCK
    n_blk = s // blk
    gains = [jnp.tile(g.astype(F32), LANES // SB_HEAD_DIM).reshape(1, LANES) for g in (q_gain, k_gain)]
    col_block = lambda off: pl.BlockSpec((1, s, LANES), lambda bi, hp: (bi, 0, off + hp))
    return pl.pallas_call(
        functools.partial(_sb_attn_kernel, blk=blk),
        grid=(b, n_pair),
        in_specs=[col_block(0), col_block(n_pair), col_block(2 * n_pair),
                  _resident((1, LANES)), _resident((1, LANES))],
        out_specs=pl.BlockSpec((1, s, LANES), lambda bi, hp: (bi, 0, hp)),
        out_shape=jax.ShapeDtypeStruct((b, s, d), BF16),
        scratch_shapes=[
            pltpu.VMEM((s, LANES), BF16),
            pltpu.VMEM((n_blk, 2 * blk, LANES), BF16),
            pltpu.VMEM((n_blk, 2 * blk, LANES), BF16),
            pltpu.VMEM((2, blk, LANES), F32),
            pltpu.VMEM((2, 2, blk, 1), F32),
        ],
        compiler_params=_params(2),
        name=name,
    )(qkv, qkv, qkv, *gains)


def _pool_kernel(h_ref, g_ref, win_ref, wgrp_ref, sc_ref, o_ref, ext_s, *, tm, seq):
    i = pl.program_id(0)
    t0 = (i * tm) % seq

    @pl.when(t0 == 0)
    def _():
        ext_s[0:POOL_HALO, :] = jnp.zeros((POOL_HALO, ext_s.shape[1]), F32)

    x = h_ref[...]
    u = _dot(_rms(x, g_ref[...]).astype(BF16), win_ref[...])
    ext_s[POOL_HALO:POOL_HALO + tm, :] = u

    pos = t0 + lax.broadcasted_iota(jnp.int32, (tm, 1), 0)
    gdim = u.shape[1] // len(POOL_WINDOWS)
    for g, w in enumerate(POOL_WINDOWS):
        cols = slice(g * gdim, (g + 1) * gdim)
        ug = u[:, cols]
        run = ext_s[:, cols]
        span = 1
        while span < w:
            run = run + pltpu.roll(run, span, axis=0)
            span *= 2
        win_sum = run[POOL_HALO:POOL_HALO + tm, :]
        cnt = jnp.minimum(pos + 1, w).astype(F32)
        pooled = win_sum / cnt - ug
        y = _dot(pooled.astype(BF16), wgrp_ref[g]) * sc_ref[:, cols]
        o_ref[:, cols] = x[:, cols] + y

    ext_s[0:POOL_HALO, :] = ext_s[tm:tm + POOL_HALO, :]


def _pool_mixer(h, seq, g, w_in, w_grp, scale, tm, name):
    t, d = h.shape
    ng, gd, _ = w_grp.shape
    return pl.pallas_call(
        functools.partial(_pool_kernel, tm=tm, seq=seq),
        grid=(t // tm,),
        in_specs=[pl.BlockSpec((tm, d), lambda i: (i, 0)), _resident((1, d)), _resident((d, d)),
                  _resident((ng, gd, gd)), _resident((1, d))],
        out_specs=pl.BlockSpec((tm, d), lambda i: (i, 0)),
        out_shape=jax.ShapeDtypeStruct((t, d), F32),
        scratch_shapes=[pltpu.VMEM((POOL_HALO + tm, d), F32)],
        compiler_params=_params(1),
        name=name,
    )(h, g, w_in, w_grp, scale)


_HGRN_LEVELS = (64, 32, 16, 8, 4, 2, 1)


def _hgrn_kernel(q_ref, f_ref, i_ref, g_ref, lbp_ref, og_ref, o_ref, cum_s, level_s, state_s, *,
                 layer_idx):
    c = HGRN_CHUNK
    n_chunk = q_ref.shape[1] // c

    lbp = lbp_ref[...]
    e = jnp.exp(lbp - jnp.max(lbp, axis=0, keepdims=True))
    sm = e / jnp.sum(e, axis=0, keepdims=True)
    lb = jnp.sum(sm[1:layer_idx + 1], axis=0, keepdims=True)
    log_lb = jnp.log(lb)
    log_1m_lb = jnp.log1p(-lb)

    r = lax.broadcasted_iota(jnp.int32, (c, c), 0)
    s = lax.broadcasted_iota(jnp.int32, (c, c), 1)
    one = lambda m: jnp.where(m, 1.0, 0.0).astype(BF16)
    cum_s[0:c, :] = one(s <= r)
    cum_s[c:2 * c, :] = one(s > r)
    for l, m in enumerate(_HGRN_LEVELS):
        sh = m.bit_length() - 1
        same = (r >> sh) == (s >> sh)
        odd = ((r >> sh) & 1) == 1
        cum_s[(2 + l) * c:(3 + l) * c, :] = one(same & (odd == (s <= r)))

    level = jnp.full((c, c), len(_HGRN_LEVELS), jnp.int32)
    for l, m in enumerate(_HGRN_LEVELS):
        sh = m.bit_length() - 1
        level = jnp.where((((r >> sh) & 1) == 1) & ((s >> sh) == (r >> sh) - 1), l, level)
    level_s[...] = level

    state_s[...] = jnp.zeros_like(state_s)

    def chunk(n, _):
        rows = pl.ds(pl.multiple_of(n * c, c), c)
        fg = f_ref[0, rows, :].astype(F32)
        a_ = log_lb
        c_ = log_1m_lb + _log_sigmoid(fg)
        log_f = jnp.maximum(a_, c_) + jnp.log(1.0 + jnp.exp(-jnp.abs(a_ - c_)))
        k = (1.0 - lb) * jax.nn.sigmoid(-fg)
        q = _silu(q_ref[0, rows, :].astype(F32))
        v = i_ref[0, rows, :]

        cum = _dot(cum_s[...], (log_f * LOG2_E).astype(BF16))
        dec = jnp.exp2(cum)
        b_end = dec[c - 1:c, :]

        state = state_s[...]
        o = _dot_nt((q * dec[0:c]).astype(BF16), state.astype(BF16))
        kd = (k * dec[c:2 * c]).astype(BF16)
        v_t = v.astype(F32).T.astype(BF16)
        state_s[...] = state * b_end + _dot(v_t, kd)

        scores = jnp.zeros((c, c), F32)
        level = level_s[...]
        for l in range(len(_HGRN_LEVELS)):
            d_l = dec[(2 + l) * c:(3 + l) * c]
            sc = _dot_nt((q * d_l).astype(BF16), (k * d_l).astype(BF16))
            scores = jnp.where(level == l, sc, scores)
        o = o + _dot(scores.astype(BF16), v)
        o = o + jnp.sum(q * k, axis=-1, keepdims=True) * v.astype(F32)

        o = _rms(o, og_ref[...]) * _silu(g_ref[0, rows, :].astype(F32))
        o_ref[0, rows, :] = o.astype(o_ref.dtype)
        return 0

    lax.fori_loop(0, n_chunk, chunk, 0, unroll=16)


def _hgrn_core(proj, lower_bounds, layer_idx, o_gain, name):
    b, s, d4 = proj.shape
    d = d4 // 4
    n_head = d // LANES
    depth = lower_bounds.shape[0]
    col_block = lambda off: pl.BlockSpec((1, s, LANES), lambda bi, hd: (bi, 0, off + hd))
    return pl.pallas_call(
        functools.partial(_hgrn_kernel, layer_idx=layer_idx),
        grid=(b, n_head),
        in_specs=[col_block(0), col_block(n_head), col_block(2 * n_head), col_block(3 * n_head),
                  pl.BlockSpec((depth, LANES), lambda bi, hd: (0, hd)), _resident((1, LANES))],
        out_specs=pl.BlockSpec((1, s, LANES), lambda bi, hd: (bi, 0, hd)),
        out_shape=jax.ShapeDtypeStruct((b, s, d), BF16),
        scratch_shapes=[pltpu.VMEM(((2 + len(_HGRN_LEVELS)) * HGRN_CHUNK, HGRN_CHUNK), BF16),
                        pltpu.VMEM((HGRN_CHUNK, HGRN_CHUNK), jnp.int32),
                        pltpu.VMEM((LANES, LANES), F32)],
        compiler_params=_params(2),
        name=name,
    )(proj, proj, proj, proj, lower_bounds, o_gain.reshape(1, LANES))


def _swiglu_into(acc_ref, hn, wgu, wd, d_ff, ck):
    for c0 in range(0, d_ff, ck):
        c1 = min(c0 + ck, d_ff)
        gate = _dot(hn, wgu[:, c0:c1])
        up = _dot(hn, wgu[:, d_ff + c0:d_ff + c1])
        acc_ref[...] += _dot((_silu(gate) * up).astype(BF16), wd[c0:c1, :])


def _ple(h, p, gn, wg, wp):
    gate = jax.nn.sigmoid(_dot(_rms(h, gn).astype(BF16), wg))
    return h + gate * _dot(p.astype(BF16), wp)


def _dense_post_kernel(h_ref, y_ref, p_ref, wo_ref, nf_ref, wgu_ref, wd_ref, gn_ref, wg_ref, wp_ref,
                       o_ref, acc_ref, *, ck):
    h1 = h_ref[...] + _dot(y_ref[...], wo_ref[...])
    acc_ref[...] = h1
    hn = _rms(h1, nf_ref[...]).astype(BF16)
    _swiglu_into(acc_ref, hn, wgu_ref, wd_ref, wd_ref.shape[0], ck)
    o_ref[...] = _ple(acc_ref[...], p_ref[...], gn_ref[...], wg_ref[...], wp_ref[...])


def _layer_tile(p_all, layer, tm):
    return pl.BlockSpec((None, tm, p_all.shape[2]), lambda i: (layer, i, 0))


def _dense_post(h, y, p_all, layer, wo, nf, wgu, wd, gn, wg, wp, tm, name):
    t, d = h.shape
    tile = lambda n: pl.BlockSpec((tm, n), lambda i: (i, 0))
    weights = (wo, nf, wgu, wd, gn, wg, wp)
    return pl.pallas_call(
        functools.partial(_dense_post_kernel, ck=256),
        grid=(t // tm,),
        in_specs=[tile(d), tile(d), _layer_tile(p_all, layer, tm)] + [_resident(w.shape) for w in weights],
        out_specs=tile(d),
        out_shape=jax.ShapeDtypeStruct((t, d), F32),
        scratch_shapes=[pltpu.VMEM((tm, d), F32)],
        compiler_params=_params(1),
        name=name,
    )(h, y, p_all, *weights)


def _pack_halves(x):
    n = x.shape[1] // 2
    bits = lambda v: lax.bitcast_convert_type(v.astype(BF16).astype(F32), jnp.int32)
    return bits(x[:, n:]) | lax.shift_right_logical(bits(x[:, :n]), 16)


def _unpack_halves(u):
    lo = lax.bitcast_convert_type(lax.shift_left(u, 16), F32)
    hi = lax.bitcast_convert_type(u & jnp.int32(-65536), F32)
    return jnp.concatenate([lo.astype(BF16), hi.astype(BF16)], axis=1)


def _route_tail(h1, nf_ref, wr_ref, cnt_s, hnp_ref, sel_ref, gate_ref, cnt_ref):
    @pl.when(pl.program_id(0) == 0)
    def _():
        cnt_s[...] = jnp.zeros_like(cnt_s)

    hn = _rms(h1, nf_ref[...]).astype(BF16)
    hnp_ref[...] = _pack_halves(hn)
    logits = _dot(hn, wr_ref[...])
    tm, n_exp = logits.shape
    idx = lax.broadcasted_iota(jnp.int32, logits.shape, 1)
    m1 = jnp.max(logits, axis=-1, keepdims=True)
    i1 = jnp.min(jnp.where(logits == m1, idx, n_exp), axis=-1, keepdims=True)
    rest = jnp.where(idx == i1, -jnp.inf, logits)
    m2 = jnp.max(rest, axis=-1, keepdims=True)
    i2 = jnp.min(jnp.where(rest == m2, idx, n_exp), axis=-1, keepdims=True)
    e2 = jnp.exp(m2 - m1)
    g1 = 1.0 / (1.0 + e2)

    onehot = jnp.where(idx == i1, 1.0, jnp.where(idx == i2, 1.0, 0.0))
    r = lax.broadcasted_iota(jnp.int32, (tm, tm), 0)
    c = lax.broadcasted_iota(jnp.int32, (tm, tm), 1)
    before = cnt_s[...] + _dot(jnp.where(c < r, 1.0, 0.0).astype(BF16), onehot.astype(BF16))
    rank1 = jnp.sum(jnp.where(idx == i1, before, 0.0), axis=-1, keepdims=True).astype(jnp.int32)
    rank2 = jnp.sum(jnp.where(idx == i2, before, 0.0), axis=-1, keepdims=True).astype(jnp.int32)
    cnt_s[...] += jnp.sum(onehot, axis=0, keepdims=True)
    cnt_ref[...] = cnt_s[...]

    lane = lax.broadcasted_iota(jnp.int32, sel_ref.shape, 1)
    sel_ref[...] = jnp.where(lane == 0, i1, jnp.where(lane == 1, i2, jnp.where(lane == 2, rank1, rank2)))
    gate_ref[...] = jnp.where(lax.broadcasted_iota(jnp.int32, gate_ref.shape, 1) == 0, g1, e2 * g1)


def _route_proj_kernel(h_ref, y_ref, wo_ref, nf_ref, wr_ref, h1_ref, hnp_ref, sel_ref, gate_ref, cnt_ref,
                       cnt_s):
    h1 = h_ref[...] + _dot(y_ref[...], wo_ref[...])
    h1_ref[...] = h1
    _route_tail(h1, nf_ref, wr_ref, cnt_s, hnp_ref, sel_ref, gate_ref, cnt_ref)


def _route_kernel(h_ref, nf_ref, wr_ref, hnp_ref, sel_ref, gate_ref, cnt_ref, cnt_s):
    _route_tail(h_ref[...], nf_ref, wr_ref, cnt_s, hnp_ref, sel_ref, gate_ref, cnt_ref)


def _route(h, y, wo, nf, wr, tm, name):
    t, d = h.shape
    n_exp = wr.shape[1]
    tile = lambda n: pl.BlockSpec((tm, n), lambda i: (i, 0))
    outs = [jax.ShapeDtypeStruct((t, d // 2), jnp.int32), jax.ShapeDtypeStruct((t, 4), jnp.int32),
            jax.ShapeDtypeStruct((t, TOP_K), F32), jax.ShapeDtypeStruct((1, n_exp), F32)]
    out_specs = [tile(d // 2), tile(4), tile(TOP_K), _resident((1, n_exp))]
    scratch = [pltpu.VMEM((1, n_exp), F32)]
    if y is None:
        res = pl.pallas_call(
            _route_kernel, grid=(t // tm,),
            in_specs=[tile(d), _resident(nf.shape), _resident(wr.shape)],
            out_specs=out_specs, out_shape=outs, scratch_shapes=scratch,
            compiler_params=_params(1), name=name,
        )(h, nf, wr)
        return (h, *res)
    return pl.pallas_call(
        _route_proj_kernel, grid=(t // tm,),
        in_specs=[tile(d), tile(d), _resident(wo.shape), _resident(nf.shape), _resident(wr.shape)],
        out_specs=[tile(d)] + out_specs,
        out_shape=[jax.ShapeDtypeStruct((t, d), F32)] + outs, scratch_shapes=scratch,
        compiler_params=_params(1), name=name,
    )(h, y, wo, nf, wr)


def _sc_mesh():
    return plsc.VectorSubcoreMesh(core_axis_name="c", subcore_axis_name="s")


def _sc_worker():
    return lax.axis_index("s") * SC_CORES + lax.axis_index("c")


def _sc_scatter_rows(x, idx, n_out):
    r, d = x.shape
    per_w = idx.shape[0] // SC_WORKERS
    n_chunk = per_w // SC_CHUNK

    @functools.partial(
        pl.kernel, mesh=_sc_mesh(), out_type=jax.ShapeDtypeStruct((n_out, d), x.dtype),
        scratch_types=[pltpu.VMEM((n_chunk, SC_CHUNK), jnp.int32), pltpu.VMEM((2, SC_CHUNK, d), x.dtype),
                       pltpu.SemaphoreType.DMA((2,))])
    def scatter(x_hbm, idx_hbm, out_hbm, idx_v, rows_v, sem):
        wid = _sc_worker()
        pltpu.sync_copy(idx_hbm.at[wid], idx_v)

        def put(j, slot):
            return pltpu.make_async_copy(rows_v.at[slot], out_hbm.at[idx_v.at[j]], sem.at[slot])

        @pl.loop(0, n_chunk, step=2)
        def _(j):
            for slot in range(2):
                jj = j + slot

                @pl.when(jj >= 2)
                def _():
                    put(jj - 2, slot).wait()

                src = pl.multiple_of((wid * per_w + jj * SC_CHUNK) % r, SC_CHUNK)
                pltpu.sync_copy(x_hbm.at[pl.ds(src, SC_CHUNK)], rows_v.at[slot])
                put(jj, slot).start()

        put(n_chunk - 2, 0).wait()
        put(n_chunk - 1, 1).wait()

    return scatter(x, idx.reshape(SC_WORKERS, n_chunk, SC_CHUNK))


def _sc_gather_rows(table, idx):
    d = table.shape[1]
    b = idx.shape[0]
    per_w = b // SC_WORKERS
    n_chunk = per_w // SC_CHUNK

    @functools.partial(
        pl.kernel, mesh=_sc_mesh(), out_type=jax.ShapeDtypeStruct((b, d), table.dtype),
        scratch_types=[pltpu.VMEM((n_chunk, SC_CHUNK), jnp.int32), pltpu.VMEM((2, SC_CHUNK, d), table.dtype),
                       pltpu.SemaphoreType.DMA((2,))])
    def gather(table_hbm, idx_hbm, out_hbm, idx_v, rows_v, sem):
        wid = _sc_worker()
        pltpu.sync_copy(idx_hbm.at[wid], idx_v)

        def fetch(j, slot):
            return pltpu.make_async_copy(table_hbm.at[idx_v.at[j]], rows_v.at[slot], sem.at[slot])

        fetch(0, 0).start()

        @pl.loop(0, n_chunk, step=2)
        def _(j):
            for slot in range(2):
                jj = j + slot
                fetch(jj, slot).wait()

                @pl.when(jj + 1 < n_chunk)
                def _():
                    fetch(jj + 1, 1 - slot).start()

                off = pl.multiple_of(wid * per_w + jj * SC_CHUNK, SC_CHUNK)
                pltpu.sync_copy(rows_v.at[slot], out_hbm.at[pl.ds(off, SC_CHUNK)])

    return gather(table, idx.reshape(SC_WORKERS, n_chunk, SC_CHUNK))


def _expert_ffn_kernel(te_ref, nu_ref, xs_ref, wgu_ref, wd_ref, ys_ref, acc_ref, wgu_bf, wd_bf, *, ck):
    j = pl.program_id(0)

    @pl.when(j < nu_ref[0])
    def _():
        @pl.when(jnp.logical_or(j == 0, te_ref[j] != te_ref[jnp.maximum(j - 1, 0)]))
        def _():
            for c0 in range(0, wgu_bf.shape[1], ck):
                wgu_bf[:, c0:c0 + ck] = wgu_ref[:, c0:c0 + ck].astype(BF16)
            for c0 in range(0, wd_bf.shape[0], ck):
                c1 = min(c0 + ck, wd_bf.shape[0])
                wd_bf[c0:c1, :] = wd_ref[c0:c1, :].astype(BF16)

        acc_ref[...] = jnp.zeros_like(acc_ref)
        _swiglu_into(acc_ref, _unpack_halves(xs_ref[...]), wgu_bf, wd_bf, wd_bf.shape[0], ck)
        ys_ref[...] = _pack_halves(acc_ref[...])


def _expert_ffn(xs, tile_expert, n_used, wgu_all, wd_all, moe_idx, tr, name):
    n_rows, half = xs.shape
    _, _, d_ffe, d = wd_all.shape
    return pl.pallas_call(
        functools.partial(_expert_ffn_kernel, ck=256),
        grid_spec=pltpu.PrefetchScalarGridSpec(
            num_scalar_prefetch=2,
            grid=(n_rows // tr,),
            in_specs=[pl.BlockSpec((tr, half), lambda j, te, nu: (j, 0)),
                      pl.BlockSpec((None, None, d, 2 * d_ffe), lambda j, te, nu: (moe_idx, te[j], 0, 0)),
                      pl.BlockSpec((None, None, d_ffe, d), lambda j, te, nu: (moe_idx, te[j], 0, 0))],
            out_specs=pl.BlockSpec((tr, half), lambda j, te, nu: (j, 0)),
            scratch_shapes=[pltpu.VMEM((tr, d), F32), pltpu.VMEM((d, 2 * d_ffe), BF16),
                            pltpu.VMEM((d_ffe, d), BF16)]),
        out_shape=jax.ShapeDtypeStruct((n_rows, half), jnp.int32),
        compiler_params=_params(1),
        name=name,
    )(tile_expert, n_used, xs, wgu_all, wd_all)


def _combine_kernel(h1_ref, y1_ref, y2_ref, gate_ref, p_ref, gn_ref, wg_ref, wp_ref, o_ref):
    gate = gate_ref[...]
    h2 = (h1_ref[...] + gate[:, 0:1] * _unpack_halves(y1_ref[...]).astype(F32)
          + gate[:, 1:2] * _unpack_halves(y2_ref[...]).astype(F32))
    o_ref[...] = _ple(h2, p_ref[...], gn_ref[...], wg_ref[...], wp_ref[...])


def _combine(h1, yg, gate, p_all, layer, gn, wg, wp, tm, name):
    t, d = h1.shape
    n_tile = t // tm
    tile = lambda n: pl.BlockSpec((tm, n), lambda i: (i, 0))
    return pl.pallas_call(
        _combine_kernel, grid=(n_tile,),
        in_specs=[tile(d), tile(d // 2), pl.BlockSpec((tm, d // 2), lambda i: (i + n_tile, 0)),
                  tile(TOP_K), _layer_tile(p_all, layer, tm),
                  _resident(gn.shape), _resident(wg.shape), _resident(wp.shape)],
        out_specs=tile(d),
        out_shape=jax.ShapeDtypeStruct((t, d), F32),
        compiler_params=_params(1),
        name=name,
    )(h1, yg, yg, gate, p_all, gn, wg, wp)


def _moe_post(h1, hnp, sel, gate, counts, p_all, wgu_all, wd_all, moe_idx, gn, wg, wp, tm, layer):
    t = h1.shape[0]
    n_exp = wd_all.shape[1]
    tr = EXPERT_ROW_TILE
    n_rows = TOP_K * t + n_exp * tr
    cnt = counts[0].astype(jnp.int32)
    padded = (cnt + tr - 1) // tr * tr
    ends = jnp.cumsum(padded)
    starts = ends - padded
    e12, rank = sel[:, :TOP_K], sel[:, TOP_K:]
    pos = jnp.sum(jnp.where(e12[:, :, None] == jnp.arange(n_exp), starts, 0), axis=-1) + rank
    pos = pos.T.reshape(-1)
    n_used = (ends[-1] // tr).reshape(1)
    tile_start = jnp.arange(n_rows // tr, dtype=jnp.int32) * tr
    tile_expert = jnp.minimum(jnp.sum(tile_start[:, None] >= ends[None, :], axis=-1), n_exp - 1)

    xs = _sc_scatter_rows(hnp, pos, n_rows)
    ys = _expert_ffn(xs, tile_expert.astype(jnp.int32), n_used.astype(jnp.int32), wgu_all, wd_all, moe_idx, tr,
                     f"experts_{layer}")
    yg = _sc_gather_rows(ys, pos)
    return _combine(h1, yg, gate, p_all, layer, gn, wg, wp, tm, f"combine_{layer}")


def kernel(x, p, norm_mix, norm_ffn, sb_w_qkv, sb_q_gain, sb_k_gain, sb_w_o, pool_w_in, pool_w_grp,
           pool_scale, hgrn_w_in, hgrn_lower_bounds, hgrn_o_gain, hgrn_w_o, ffn_w_gu, ffn_w_d,
           moe_router, moe_w_gu, moe_w_d, ple_w, ple_gate_norm, ple_gate_w):
    b, s, d = x.shape
    depth = p.shape[0]
    t = b * s
    tm = 512
    bf = lambda w: w.astype(BF16)
    vec = lambda g: g.reshape(1, -1)

    h = x.reshape(t, d)
    p_all = p.reshape(depth, t, -1)
    for i in range(depth):
        kind, j = i % N_MIXERS, i // N_MIXERS
        ple = (vec(ple_gate_norm[i]), bf(ple_gate_w[i]), bf(ple_w[i]))
        y = w_o = None
        if kind == 0:
            qkv = _norm_matmul(h, vec(norm_mix[i]), bf(sb_w_qkv[j]), BF16, tm, f"sb_qkv_{i}")
            y = _sb_attention(qkv.reshape(b, s, 3 * d), sb_q_gain[j], sb_k_gain[j], f"sb_attn_{i}")
            y, w_o = y.reshape(t, d), bf(sb_w_o[j])
        elif kind == 1:
            h = _pool_mixer(h, s, vec(norm_mix[i]), bf(pool_w_in[j]), bf(pool_w_grp[j]),
                            vec(pool_scale[j]), tm, f"pool_{i}")
        else:
            proj = _norm_matmul(h, vec(norm_mix[i]), bf(hgrn_w_in[j]), BF16, tm, f"hgrn_in_{i}")
            y = _hgrn_core(proj.reshape(b, s, 4 * d), hgrn_lower_bounds, i, hgrn_o_gain[j], f"hgrn_{i}")
            y, w_o = y.reshape(t, d), bf(hgrn_w_o[j])

        c = i // 2
        if i % 2 == 0:
            h = _dense_post(h, y, p_all, i, w_o, vec(norm_ffn[i]), bf(ffn_w_gu[c]), bf(ffn_w_d[c]), *ple,
                            tm, f"dense_{i}")
        else:
            routed = _route(h, y, w_o, vec(norm_ffn[i]), bf(moe_router[c]), tm, f"route_{i}")
            h = _moe_post(*routed, p_all, moe_w_gu, moe_w_d, c, *ple, tm, i)
    return h.reshape(b, s, d)
```

```python
import functools

import jax
import jax.numpy as jnp
from jax import lax
from jax.experimental import pallas as pl
from jax.experimental.pallas import tpu as pltpu
from jax.experimental.pallas import tpu_sc as plsc

F32 = jnp.float32
BF16 = jnp.bfloat16

EPS = 1e-6
LOG2_E = 1.4426950408889634
N_MIXERS = 3
SB_HEAD_DIM = 64
POOL_WINDOWS = (2, 4, 8, 16)
HGRN_CHUNK = 128
TOP_K = 2

LANES = 128
SB_BLOCK = 256
POOL_HALO = 16
EXPERT_ROW_TILE = 512
SC_CORES = 2
SC_WORKERS = SC_CORES * 16
SC_CHUNK = 64
VMEM_LIMIT = 56 * 1024 * 1024
LOG_ZERO_F32 = -104.0


def _dot(a, b):
    return jnp.dot(a, b, preferred_element_type=F32)


def _dot_nt(a, b):
    return lax.dot_general(a, b, (((1,), (1,)), ((), ())), preferred_element_type=F32)


def _split_bf16(x):
    hi = x.astype(BF16)
    lo = (x - hi.astype(F32)).astype(BF16)
    return hi, lo


def _rms(x, g):
    ms = jnp.mean(x * x, axis=-1, keepdims=True)
    return x * lax.rsqrt(ms + EPS) * g


def _log_sigmoid(x):
    return -(jnp.maximum(-x, 0.0) + jnp.log(1.0 + jnp.exp(-jnp.abs(x))))


def _silu(x):
    return x * jax.nn.sigmoid(x)


def _params(n_axes):
    return pltpu.CompilerParams(
        dimension_semantics=("arbitrary",) * n_axes, vmem_limit_bytes=VMEM_LIMIT)


def _resident(shape):
    nd = len(shape)
    return pl.BlockSpec(shape, lambda *_: (0,) * nd)


def _norm_matmul_kernel(h_ref, g_ref, w_ref, o_ref, *, n_chunk):
    xn = _rms(h_ref[...], g_ref[...]).astype(BF16)
    for c0 in range(0, w_ref.shape[1], n_chunk):
        o_ref[:, c0:c0 + n_chunk] = _dot(xn, w_ref[:, c0:c0 + n_chunk]).astype(o_ref.dtype)


def _norm_matmul(h, g, w, out_dtype, tm, name):
    t, d = h.shape
    n = w.shape[1]
    return pl.pallas_call(
        functools.partial(_norm_matmul_kernel, n_chunk=512),
        grid=(t // tm,),
        in_specs=[pl.BlockSpec((tm, d), lambda i: (i, 0)), _resident((1, d)), _resident((d, n))],
        out_specs=pl.BlockSpec((tm, n), lambda i: (i, 0)),
        out_shape=jax.ShapeDtypeStruct((t, n), out_dtype),
        compiler_params=_params(1),
        name=name,
    )(h, g, w)


def _sb_attn_kernel(q_ref, k_ref, v_ref, qg_ref, kg_ref, o_ref,
                    qn_s, kcat_s, vcat_s, oacc_s, carry_s, *, blk):
    s_len = q_ref.shape[1]
    n_blk = s_len // blk
    scale = SB_HEAD_DIM ** -0.5

    row = lax.broadcasted_iota(jnp.int32, (LANES, LANES), 0)
    col = lax.broadcasted_iota(jnp.int32, (LANES, LANES), 1)
    head_sum = jnp.where((row >> 6) == (col >> 6), 1.0, 0.0).astype(BF16)
    head0 = lax.broadcasted_iota(jnp.int32, (blk, LANES), 1) < SB_HEAD_DIM

    def head_norm(x, gain):
        hi, lo = _split_bf16(x * x)
        ss = _dot(hi, head_sum) + _dot(lo, head_sum)
        return x * lax.rsqrt(ss * (1.0 / SB_HEAD_DIM) + EPS) * gain

    def prep(c, _):
        r = pl.ds(pl.multiple_of(c * blk, blk), blk)
        qn = head_norm(q_ref[0, r, :].astype(F32), qg_ref[...] * (scale * LOG2_E))
        qn_s[r, :] = qn.astype(BF16)
        kn = head_norm(k_ref[0, r, :].astype(F32), kg_ref[...]).astype(BF16)
        vv = v_ref[0, r, :]
        zero = jnp.zeros_like(kn)
        kcat_s[c] = jnp.concatenate([jnp.where(head0, kn, zero), jnp.where(head0, zero, kn)], axis=0)
        vcat_s[c] = jnp.concatenate([jnp.where(head0, vv, zero), jnp.where(head0, zero, vv)], axis=0)
        return 0

    lax.fori_loop(0, n_blk, prep, 0, unroll=4)

    r2 = lax.broadcasted_iota(jnp.int32, (blk, blk), 0)
    c2 = lax.broadcasted_iota(jnp.int32, (blk, blk), 1)
    cum_mat = jnp.where(r2 > c2, 1.0, 0.0).astype(BF16)
    causal = (lax.broadcasted_iota(jnp.int32, (blk, 2 * blk), 1) & (blk - 1)) < \
        lax.broadcasted_iota(jnp.int32, (blk, 2 * blk), 0)

    def step(q_blk, kb, carry, oacc, diagonal):
        a_heads, new_carry = [], []
        for hd in range(2):
            cols = slice(hd * blk, (hd + 1) * blk)
            z = _dot_nt(q_blk, kcat_s[kb, cols, :])
            nlk = jnp.maximum(z, 0.0) + jnp.log2(1.0 + jnp.exp2(-jnp.abs(z)))
            if diagonal:
                nlk = jnp.where(causal[:, :blk], nlk, 0.0)
            later = _dot(nlk.astype(BF16), cum_mat)
            a = jnp.exp2(z - nlk - later - carry[hd])
            if diagonal:
                a = jnp.where(causal[:, :blk], a, 0.0)
            a_heads.append(a.astype(BF16))
            new_carry.append(carry[hd] + (later[:, 0:1] + nlk[:, 0:1]))
        oacc = oacc + _dot(jnp.concatenate(a_heads, axis=1), vcat_s[kb])
        return new_carry, oacc

    rows = lambda qb: pl.ds(pl.multiple_of(qb * blk, blk), blk)
    zero_carry = [jnp.zeros((blk, 1), F32)] * 2
    zero_acc = jnp.zeros((blk, LANES), F32)

    def head_blocks(qb, slot):
        q_blk = qn_s[rows(qb), :]
        carry, oacc = step(q_blk, qb, zero_carry, zero_acc, True)
        carry, oacc = step(q_blk, qb - 1, carry, oacc, False)
        carry_s[slot, 0], carry_s[slot, 1] = carry
        oacc_s[slot] = oacc
        return q_blk, jnp.min(jnp.minimum(carry[0], carry[1]))

    def tail_blocks(qb, slot, q_blk, low):
        def kv_cond(st):
            kb, low = st
            return jnp.logical_and(kb >= 0, low < -LOG_ZERO_F32 * LOG2_E)

        def kv_step(st):
            kb, _ = st
            carry, oacc = step(q_blk, kb, [carry_s[slot, 0], carry_s[slot, 1]], oacc_s[slot], False)
            carry_s[slot, 0], carry_s[slot, 1] = carry
            oacc_s[slot] = oacc
            return kb - 1, jnp.min(jnp.minimum(carry[0], carry[1]))

        lax.while_loop(kv_cond, kv_step, (qb - 2, low))
        o_ref[0, rows(qb), :] = oacc_s[slot].astype(o_ref.dtype)

    _, oacc0 = step(qn_s[rows(0), :], 0, zero_carry, zero_acc, True)
    o_ref[0, rows(0), :] = oacc0.astype(o_ref.dtype)

    def q_pair(i, _):
        qa = 1 + 2 * i
        blk_a, low_a = head_blocks(qa, 0)
        blk_b, low_b = head_blocks(qa + 1, 1)
        tail_blocks(qa, 0, blk_a, low_a)
        tail_blocks(qa + 1, 1, blk_b, low_b)
        return 0

    n_pair_trips = (n_blk - 1) // 2
    lax.fori_loop(0, n_pair_trips, q_pair, 0)
    if (n_blk - 1) % 2:
        last = n_blk - 1
        blk_l, low_l = head_blocks(last, 0)
        tail_blocks(last, 0, blk_l, low_l)


def _sb_attention(qkv, q_gain, k_gain, name):
    b, s, d3 = qkv.shape
    d = d3 // 3
    n_pair = d // LANES
    blk = SB_BLOCK
    n_blk = s // blk
    gains = [jnp.tile(g.astype(F32), LANES // SB_HEAD_DIM).reshape(1, LANES) for g in (q_gain, k_gain)]
    col_block = lambda off: pl.BlockSpec((1, s, LANES), lambda bi, hp: (bi, 0, off + hp))
    return pl.pallas_call(
        functools.partial(_sb_attn_kernel, blk=blk),
        grid=(b, n_pair),
        in_specs=[col_block(0), col_block(n_pair), col_block(2 * n_pair),
                  _resident((1, LANES)), _resident((1, LANES))],
        out_specs=pl.BlockSpec((1, s, LANES), lambda bi, hp: (bi, 0, hp)),
        out_shape=jax.ShapeDtypeStruct((b, s, d), BF16),
        scratch_shapes=[
            pltpu.VMEM((s, LANES), BF16),
            pltpu.VMEM((n_blk, 2 * blk, LANES), BF16),
            pltpu.VMEM((n_blk, 2 * blk, LANES), BF16),
            pltpu.VMEM((2, blk, LANES), F32),
            pltpu.VMEM((2, 2, blk, 1), F32),
        ],
        compiler_params=_params(2),
        name=name,
    )(qkv, qkv, qkv, *gains)


def _pool_kernel(h_ref, g_ref, win_ref, wgrp_ref, sc_ref, o_ref, ext_s, *, tm, seq):
    i = pl.program_id(0)
    t0 = (i * tm) % seq

    @pl.when(t0 == 0)
    def _():
        ext_s[0:POOL_HALO, :] = jnp.zeros((POOL_HALO, ext_s.shape[1]), F32)

    x = h_ref[...]
    u = _dot(_rms(x, g_ref[...]).astype(BF16), win_ref[...])
    ext_s[POOL_HALO:POOL_HALO + tm, :] = u

    pos = t0 + lax.broadcasted_iota(jnp.int32, (tm, 1), 0)
    gdim = u.shape[1] // len(POOL_WINDOWS)
    for g, w in enumerate(POOL_WINDOWS):
        cols = slice(g * gdim, (g + 1) * gdim)
        ug = u[:, cols]
        run = ext_s[:, cols]
        span = 1
        while span < w:
            run = run + pltpu.roll(run, span, axis=0)
            span *= 2
        win_sum = run[POOL_HALO:POOL_HALO + tm, :]
        cnt = jnp.minimum(pos + 1, w).astype(F32)
        pooled = win_sum / cnt - ug
        y = _dot(pooled.astype(BF16), wgrp_ref[g]) * sc_ref[:, cols]
        o_ref[:, cols] = x[:, cols] + y

    ext_s[0:POOL_HALO, :] = ext_s[tm:tm + POOL_HALO, :]


def _pool_mixer(h, seq, g, w_in, w_grp, scale, tm, name):
    t, d = h.shape
    ng, gd, _ = w_grp.shape
    return pl.pallas_call(
        functools.partial(_pool_kernel, tm=tm, seq=seq),
        grid=(t // tm,),
        in_specs=[pl.BlockSpec((tm, d), lambda i: (i, 0)), _resident((1, d)), _resident((d, d)),
                  _resident((ng, gd, gd)), _resident((1, d))],
        out_specs=pl.BlockSpec((tm, d), lambda i: (i, 0)),
        out_shape=jax.ShapeDtypeStruct((t, d), F32),
        scratch_shapes=[pltpu.VMEM((POOL_HALO + tm, d), F32)],
        compiler_params=_params(1),
        name=name,
    )(h, g, w_in, w_grp, scale)


_HGRN_LEVELS = (64, 32, 16, 8, 4, 2, 1)


def _hgrn_kernel(q_ref, f_ref, i_ref, g_ref, lbp_ref, og_ref, o_ref, cum_s, level_s, state_s, *,
                 layer_idx):
    c = HGRN_CHUNK
    n_chunk = q_ref.shape[1] // c

    lbp = lbp_ref[...]
    e = jnp.exp(lbp - jnp.max(lbp, axis=0, keepdims=True))
    sm = e / jnp.sum(e, axis=0, keepdims=True)
    lb = jnp.sum(sm[1:layer_idx + 1], axis=0, keepdims=True)
    log_lb = jnp.log(lb)
    log_1m_lb = jnp.log1p(-lb)

    r = lax.broadcasted_iota(jnp.int32, (c, c), 0)
    s = lax.broadcasted_iota(jnp.int32, (c, c), 1)
    one = lambda m: jnp.where(m, 1.0, 0.0).astype(BF16)
    cum_s[0:c, :] = one(s <= r)
    cum_s[c:2 * c, :] = one(s > r)
    for l, m in enumerate(_HGRN_LEVELS):
        sh = m.bit_length() - 1
        same = (r >> sh) == (s >> sh)
        odd = ((r >> sh) & 1) == 1
        cum_s[(2 + l) * c:(3 + l) * c, :] = one(same & (odd == (s <= r)))

    level = jnp.full((c, c), len(_HGRN_LEVELS), jnp.int32)
    for l, m in enumerate(_HGRN_LEVELS):
        sh = m.bit_length() - 1
        level = jnp.where((((r >> sh) & 1) == 1) & ((s >> sh) == (r >> sh) - 1), l, level)
    level_s[...] = level

    state_s[...] = jnp.zeros_like(state_s)

    def chunk(n, _):
        rows = pl.ds(pl.multiple_of(n * c, c), c)
        fg = f_ref[0, rows, :].astype(F32)
        a_ = log_lb
        c_ = log_1m_lb + _log_sigmoid(fg)
        log_f = jnp.maximum(a_, c_) + jnp.log(1.0 + jnp.exp(-jnp.abs(a_ - c_)))
        k = (1.0 - lb) * jax.nn.sigmoid(-fg)
        q = _silu(q_ref[0, rows, :].astype(F32))
        v = i_ref[0, rows, :]

        cum = _dot(cum_s[...], (log_f * LOG2_E).astype(BF16))
        dec = jnp.exp2(cum)
        b_end = dec[c - 1:c, :]

        state = state_s[...]
        o = _dot_nt((q * dec[0:c]).astype(BF16), state.astype(BF16))
        kd = (k * dec[c:2 * c]).astype(BF16)
        v_t = v.astype(F32).T.astype(BF16)
        state_s[...] = state * b_end + _dot(v_t, kd)

        scores = jnp.zeros((c, c), F32)
        level = level_s[...]
        for l in range(len(_HGRN_LEVELS)):
            d_l = dec[(2 + l) * c:(3 + l) * c]
            sc = _dot_nt((q * d_l).astype(BF16), (k * d_l).astype(BF16))
            scores = jnp.where(level == l, sc, scores)
        o = o + _dot(scores.astype(BF16), v)
        o = o + jnp.sum(q * k, axis=-1, keepdims=True) * v.astype(F32)

        o = _rms(o, og_ref[...]) * _silu(g_ref[0, rows, :].astype(F32))
        o_ref[0, rows, :] = o.astype(o_ref.dtype)
        return 0

    lax.fori_loop(0, n_chunk, chunk, 0, unroll=16)


def _hgrn_core(proj, lower_bounds, layer_idx, o_gain, name):
    b, s, d4 = proj.shape
    d = d4 // 4
    n_head = d // LANES
    depth = lower_bounds.shape[0]
    col_block = lambda off: pl.BlockSpec((1, s, LANES), lambda bi, hd: (bi, 0, off + hd))
    return pl.pallas_call(
        functools.partial(_hgrn_kernel, layer_idx=layer_idx),
        grid=(b, n_head),
        in_specs=[col_block(0), col_block(n_head), col_block(2 * n_head), col_block(3 * n_head),
                  pl.BlockSpec((depth, LANES), lambda bi, hd: (0, hd)), _resident((1, LANES))],
        out_specs=pl.BlockSpec((1, s, LANES), lambda bi, hd: (bi, 0, hd)),
        out_shape=jax.ShapeDtypeStruct((b, s, d), BF16),
        scratch_shapes=[pltpu.VMEM(((2 + len(_HGRN_LEVELS)) * HGRN_CHUNK, HGRN_CHUNK), BF16),
                        pltpu.VMEM((HGRN_CHUNK, HGRN_CHUNK), jnp.int32),
                        pltpu.VMEM((LANES, LANES), F32)],
        compiler_params=_params(2),
        name=name,
    )(proj, proj, proj, proj, lower_bounds, o_gain.reshape(1, LANES))


def _swiglu_into(acc_ref, hn, wgu, wd, d_ff, ck):
    for c0 in range(0, d_ff, ck):
        c1 = min(c0 + ck, d_ff)
        gate = _dot(hn, wgu[:, c0:c1])
        up = _dot(hn, wgu[:, d_ff + c0:d_ff + c1])
        acc_ref[...] += _dot((_silu(gate) * up).astype(BF16), wd[c0:c1, :])


def _ple(h, p, gn, wg, wp):
    gate = jax.nn.sigmoid(_dot(_rms(h, gn).astype(BF16), wg))
    return h + gate * _dot(p.astype(BF16), wp)


def _dense_post_kernel(h_ref, y_ref, p_ref, wo_ref, nf_ref, wgu_ref, wd_ref, gn_ref, wg_ref, wp_ref,
                       o_ref, acc_ref, *, ck):
    h1 = h_ref[...] + _dot(y_ref[...], wo_ref[...])
    acc_ref[...] = h1
    hn = _rms(h1, nf_ref[...]).astype(BF16)
    _swiglu_into(acc_ref, hn, wgu_ref, wd_ref, wd_ref.shape[0], ck)
    o_ref[...] = _ple(acc_ref[...], p_ref[...], gn_ref[...], wg_ref[...], wp_ref[...])


def _layer_tile(p_all, layer, tm):
    return pl.BlockSpec((None, tm, p_all.shape[2]), lambda i: (layer, i, 0))


def _dense_post(h, y, p_all, layer, wo, nf, wgu, wd, gn, wg, wp, tm, name):
    t, d = h.shape
    tile = lambda n: pl.BlockSpec((tm, n), lambda i: (i, 0))
    weights = (wo, nf, wgu, wd, gn, wg, wp)
    return pl.pallas_call(
        functools.partial(_dense_post_kernel, ck=256),
        grid=(t // tm,),
        in_specs=[tile(d), tile(d), _layer_tile(p_all, layer, tm)] + [_resident(w.shape) for w in weights],
        out_specs=tile(d),
        out_shape=jax.ShapeDtypeStruct((t, d), F32),
        scratch_shapes=[pltpu.VMEM((tm, d), F32)],
        compiler_params=_params(1),
        name=name,
    )(h, y, p_all, *weights)


def _pack_halves(x):
    n = x.shape[1] // 2
    bits = lambda v: lax.bitcast_convert_type(v.astype(BF16).astype(F32), jnp.int32)
    return bits(x[:, n:]) | lax.shift_right_logical(bits(x[:, :n]), 16)


def _unpack_halves(u):
    lo = lax.bitcast_convert_type(lax.shift_left(u, 16), F32)
    hi = lax.bitcast_convert_type(u & jnp.int32(-65536), F32)
    return jnp.concatenate([lo.astype(BF16), hi.astype(BF16)], axis=1)


def _route_tail(h1, nf_ref, wr_ref, cnt_s, hnp_ref, sel_ref, gate_ref, cnt_ref):
    @pl.when(pl.program_id(0) == 0)
    def _():
        cnt_s[...] = jnp.zeros_like(cnt_s)

    hn = _rms(h1, nf_ref[...]).astype(BF16)
    hnp_ref[...] = _pack_halves(hn)
    logits = _dot(hn, wr_ref[...])
    tm, n_exp = logits.shape
    idx = lax.broadcasted_iota(jnp.int32, logits.shape, 1)
    m1 = jnp.max(logits, axis=-1, keepdims=True)
    i1 = jnp.min(jnp.where(logits == m1, idx, n_exp), axis=-1, keepdims=True)
    rest = jnp.where(idx == i1, -jnp.inf, logits)
    m2 = jnp.max(rest, axis=-1, keepdims=True)
    i2 = jnp.min(jnp.where(rest == m2, idx, n_exp), axis=-1, keepdims=True)
    e2 = jnp.exp(m2 - m1)
    g1 = 1.0 / (1.0 + e2)

    onehot = jnp.where(idx == i1, 1.0, jnp.where(idx == i2, 1.0, 0.0))
    r = lax.broadcasted_iota(jnp.int32, (tm, tm), 0)
    c = lax.broadcasted_iota(jnp.int32, (tm, tm), 1)
    before = cnt_s[...] + _dot(jnp.where(c < r, 1.0, 0.0).astype(BF16), onehot.astype(BF16))
    rank1 = jnp.sum(jnp.where(idx == i1, before, 0.0), axis=-1, keepdims=True).astype(jnp.int32)
    rank2 = jnp.sum(jnp.where(idx == i2, before, 0.0), axis=-1, keepdims=True).astype(jnp.int32)
    cnt_s[...] += jnp.sum(onehot, axis=0, keepdims=True)
    cnt_ref[...] = cnt_s[...]

    lane = lax.broadcasted_iota(jnp.int32, sel_ref.shape, 1)
    sel_ref[...] = jnp.where(lane == 0, i1, jnp.where(lane == 1, i2, jnp.where(lane == 2, rank1, rank2)))
    gate_ref[...] = jnp.where(lax.broadcasted_iota(jnp.int32, gate_ref.shape, 1) == 0, g1, e2 * g1)


def _route_proj_kernel(h_ref, y_ref, wo_ref, nf_ref, wr_ref, h1_ref, hnp_ref, sel_ref, gate_ref, cnt_ref,
                       cnt_s):
    h1 = h_ref[...] + _dot(y_ref[...], wo_ref[...])
    h1_ref[...] = h1
    _route_tail(h1, nf_ref, wr_ref, cnt_s, hnp_ref, sel_ref, gate_ref, cnt_ref)


def _route_kernel(h_ref, nf_ref, wr_ref, hnp_ref, sel_ref, gate_ref, cnt_ref, cnt_s):
    _route_tail(h_ref[...], nf_ref, wr_ref, cnt_s, hnp_ref, sel_ref, gate_ref, cnt_ref)


def _route(h, y, wo, nf, wr, tm, name):
    t, d = h.shape
    n_exp = wr.shape[1]
    tile = lambda n: pl.BlockSpec((tm, n), lambda i: (i, 0))
    outs = [jax.ShapeDtypeStruct((t, d // 2), jnp.int32), jax.ShapeDtypeStruct((t, 4), jnp.int32),
            jax.ShapeDtypeStruct((t, TOP_K), F32), jax.ShapeDtypeStruct((1, n_exp), F32)]
    out_specs = [tile(d // 2), tile(4), tile(TOP_K), _resident((1, n_exp))]
    scratch = [pltpu.VMEM((1, n_exp), F32)]
    if y is None:
        res = pl.pallas_call(
            _route_kernel, grid=(t // tm,),
            in_specs=[tile(d), _resident(nf.shape), _resident(wr.shape)],
            out_specs=out_specs, out_shape=outs, scratch_shapes=scratch,
            compiler_params=_params(1), name=name,
        )(h, nf, wr)
        return (h, *res)
    return pl.pallas_call(
        _route_proj_kernel, grid=(t // tm,),
        in_specs=[tile(d), tile(d), _resident(wo.shape), _resident(nf.shape), _resident(wr.shape)],
        out_specs=[tile(d)] + out_specs,
        out_shape=[jax.ShapeDtypeStruct((t, d), F32)] + outs, scratch_shapes=scratch,
        compiler_params=_params(1), name=name,
    )(h, y, wo, nf, wr)


def _sc_mesh():
    return plsc.VectorSubcoreMesh(core_axis_name="c", subcore_axis_name="s")


def _sc_worker():
    return lax.axis_index("s") * SC_CORES + lax.axis_index("c")


def _sc_scatter_rows(x, idx, n_out):
    r, d = x.shape
    per_w = idx.shape[0] // SC_WORKERS
    n_chunk = per_w // SC_CHUNK

    @functools.partial(
        pl.kernel, mesh=_sc_mesh(), out_type=jax.ShapeDtypeStruct((n_out, d), x.dtype),
        scratch_types=[pltpu.VMEM((n_chunk, SC_CHUNK), jnp.int32), pltpu.VMEM((2, SC_CHUNK, d), x.dtype),
                       pltpu.SemaphoreType.DMA((2,))])
    def scatter(x_hbm, idx_hbm, out_hbm, idx_v, rows_v, sem):
        wid = _sc_worker()
        pltpu.sync_copy(idx_hbm.at[wid], idx_v)

        def put(j, slot):
            return pltpu.make_async_copy(rows_v.at[slot], out_hbm.at[idx_v.at[j]], sem.at[slot])

        @pl.loop(0, n_chunk, step=2)
        def _(j):
            for slot in range(2):
                jj = j + slot

                @pl.when(jj >= 2)
                def _():
                    put(jj - 2, slot).wait()

                src = pl.multiple_of((wid * per_w + jj * SC_CHUNK) % r, SC_CHUNK)
                pltpu.sync_copy(x_hbm.at[pl.ds(src, SC_CHUNK)], rows_v.at[slot])
                put(jj, slot).start()

        put(n_chunk - 2, 0).wait()
        put(n_chunk - 1, 1).wait()

    return scatter(x, idx.reshape(SC_WORKERS, n_chunk, SC_CHUNK))


def _sc_gather_rows(table, idx):
    d = table.shape[1]
    b = idx.shape[0]
    per_w = b // SC_WORKERS
    n_chunk = per_w // SC_CHUNK

    @functools.partial(
        pl.kernel, mesh=_sc_mesh(), out_type=jax.ShapeDtypeStruct((b, d), table.dtype),
        scratch_types=[pltpu.VMEM((n_chunk, SC_CHUNK), jnp.int32), pltpu.VMEM((2, SC_CHUNK, d), table.dtype),
                       pltpu.SemaphoreType.DMA((2,))])
    def gather(table_hbm, idx_hbm, out_hbm, idx_v, rows_v, sem):
        wid = _sc_worker()
        pltpu.sync_copy(idx_hbm.at[wid], idx_v)

        def fetch(j, slot):
            return pltpu.make_async_copy(table_hbm.at[idx_v.at[j]], rows_v.at[slot], sem.at[slot])

        fetch(0, 0).start()

        @pl.loop(0, n_chunk, step=2)
        def _(j):
            for slot in range(2):
                jj = j + slot
                fetch(jj, slot).wait()

                @pl.when(jj + 1 < n_chunk)
                def _():
                    fetch(jj + 1, 1 - slot).start()

                off = pl.multiple_of(wid * per_w + jj * SC_CHUNK, SC_CHUNK)
                pltpu.sync_copy(rows_v.at[slot], out_hbm.at[pl.ds(off, SC_CHUNK)])

    return gather(table, idx.reshape(SC_WORKERS, n_chunk, SC_CHUNK))


def _expert_ffn_kernel(te_ref, nu_ref, xs_ref, wgu_ref, wd_ref, ys_ref, acc_ref, wgu_bf, wd_bf, *, ck):
    j = pl.program_id(0)

    @pl.when(j < nu_ref[0])
    def _():
        @pl.when(jnp.logical_or(j == 0, te_ref[j] != te_ref[jnp.maximum(j - 1, 0)]))
        def _():
            for c0 in range(0, wgu_bf.shape[1], ck):
                wgu_bf[:, c0:c0 + ck] = wgu_ref[:, c0:c0 + ck].astype(BF16)
            for c0 in range(0, wd_bf.shape[0], ck):
                c1 = min(c0 + ck, wd_bf.shape[0])
                wd_bf[c0:c1, :] = wd_ref[c0:c1, :].astype(BF16)

        acc_ref[...] = jnp.zeros_like(acc_ref)
        _swiglu_into(acc_ref, _unpack_halves(xs_ref[...]), wgu_bf, wd_bf, wd_bf.shape[0], ck)
        ys_ref[...] = _pack_halves(acc_ref[...])


def _expert_ffn(xs, tile_expert, n_used, wgu_all, wd_all, moe_idx, tr, name):
    n_rows, half = xs.shape
    _, _, d_ffe, d = wd_all.shape
    return pl.pallas_call(
        functools.partial(_expert_ffn_kernel, ck=256),
        grid_spec=pltpu.PrefetchScalarGridSpec(
            num_scalar_prefetch=2,
            grid=(n_rows // tr,),
            in_specs=[pl.BlockSpec((tr, half), lambda j, te, nu: (j, 0)),
                      pl.BlockSpec((None, None, d, 2 * d_ffe), lambda j, te, nu: (moe_idx, te[j], 0, 0)),
                      pl.BlockSpec((None, None, d_ffe, d), lambda j, te, nu: (moe_idx, te[j], 0, 0))],
            out_specs=pl.BlockSpec((tr, half), lambda j, te, nu: (j, 0)),
            scratch_shapes=[pltpu.VMEM((tr, d), F32), pltpu.VMEM((d, 2 * d_ffe), BF16),
                            pltpu.VMEM((d_ffe, d), BF16)]),
        out_shape=jax.ShapeDtypeStruct((n_rows, half), jnp.int32),
        compiler_params=_params(1),
        name=name,
    )(tile_expert, n_used, xs, wgu_all, wd_all)


def _combine_kernel(h1_ref, y1_ref, y2_ref, gate_ref, p_ref, gn_ref, wg_ref, wp_ref, o_ref):
    gate = gate_ref[...]
    h2 = (h1_ref[...] + gate[:, 0:1] * _unpack_halves(y1_ref[...]).astype(F32)
          + gate[:, 1:2] * _unpack_halves(y2_ref[...]).astype(F32))
    o_ref[...] = _ple(h2, p_ref[...], gn_ref[...], wg_ref[...], wp_ref[...])


def _combine(h1, yg, gate, p_all, layer, gn, wg, wp, tm, name):
    t, d = h1.shape
    n_tile = t // tm
    tile = lambda n: pl.BlockSpec((tm, n), lambda i: (i, 0))
    return pl.pallas_call(
        _combine_kernel, grid=(n_tile,),
        in_specs=[tile(d), tile(d // 2), pl.BlockSpec((tm, d // 2), lambda i: (i + n_tile, 0)),
                  tile(TOP_K), _layer_tile(p_all, layer, tm),
                  _resident(gn.shape), _resident(wg.shape), _resident(wp.shape)],
        out_specs=tile(d),
        out_shape=jax.ShapeDtypeStruct((t, d), F32),
        compiler_params=_params(1),
        name=name,
    )(h1, yg, yg, gate, p_all, gn, wg, wp)


def _moe_post(h1, hnp, sel, gate, counts, p_all, wgu_all, wd_all, moe_idx, gn, wg, wp, tm, layer):
    t = h1.shape[0]
    n_exp = wd_all.shape[1]
    tr = EXPERT_ROW_TILE
    n_rows = TOP_K * t + n_exp * tr
    cnt = counts[0].astype(jnp.int32)
    padded = (cnt + tr - 1) // tr * tr
    ends = jnp.cumsum(padded)
    starts = ends - padded
    e12, rank = sel[:, :TOP_K], sel[:, TOP_K:]
    pos = jnp.sum(jnp.where(e12[:, :, None] == jnp.arange(n_exp), starts, 0), axis=-1) + rank
    pos = pos.T.reshape(-1)
    n_used = (ends[-1] // tr).reshape(1)
    tile_start = jnp.arange(n_rows // tr, dtype=jnp.int32) * tr
    tile_expert = jnp.minimum(jnp.sum(tile_start[:, None] >= ends[None, :], axis=-1), n_exp - 1)

    xs = _sc_scatter_rows(hnp, pos, n_rows)
    ys = _expert_ffn(xs, tile_expert.astype(jnp.int32), n_used.astype(jnp.int32), wgu_all, wd_all, moe_idx, tr,
                     f"experts_{layer}")
    yg = _sc_gather_rows(ys, pos)
    return _combine(h1, yg, gate, p_all, layer, gn, wg, wp, tm, f"combine_{layer}")


def kernel(x, p, norm_mix, norm_ffn, sb_w_qkv, sb_q_gain, sb_k_gain, sb_w_o, pool_w_in, pool_w_grp,
           pool_scale, hgrn_w_in, hgrn_lower_bounds, hgrn_o_gain, hgrn_w_o, ffn_w_gu, ffn_w_d,
           moe_router, moe_w_gu, moe_w_d, ple_w, ple_gate_norm, ple_gate_w):
    b, s, d = x.shape
    depth = p.shape[0]
    t = b * s
    tm = 512
    bf = lambda w: w.astype(BF16)
    vec = lambda g: g.reshape(1, -1)

    h = x.reshape(t, d)
    p_all = p.reshape(depth, t, -1)
    for i in range(depth):
        kind, j = i % N_MIXERS, i // N_MIXERS
        ple = (vec(ple_gate_norm[i]), bf(ple_gate_w[i]), bf(ple_w[i]))
        y = w_o = None
        if kind == 0:
            qkv = _norm_matmul(h, vec(norm_mix[i]), bf(sb_w_qkv[j]), BF16, tm, f"sb_qkv_{i}")
            y = _sb_attention(qkv.reshape(b, s, 3 * d), sb_q_gain[j], sb_k_gain[j], f"sb_attn_{i}")
            y, w_o = y.reshape(t, d), bf(sb_w_o[j])
        elif kind == 1:
            h = _pool_mixer(h, s, vec(norm_mix[i]), bf(pool_w_in[j]), bf(pool_w_grp[j]),
                            vec(pool_scale[j]), tm, f"pool_{i}")
        else:
            proj = _norm_matmul(h, vec(norm_mix[i]), bf(hgrn_w_in[j]), BF16, tm, f"hgrn_in_{i}")
            y = _hgrn_core(proj.reshape(b, s, 4 * d), hgrn_lower_bounds, i, hgrn_o_gain[j], f"hgrn_{i}")
            y, w_o = y.reshape(t, d), bf(hgrn_w_o[j])

        c = i // 2
        if i % 2 == 0:
            h = _dense_post(h, y, p_all, i, w_o, vec(norm_ffn[i]), bf(ffn_w_gu[c]), bf(ffn_w_d[c]), *ple,
                            tm, f"dense_{i}")
        else:
            routed = _route(h, y, w_o, vec(norm_ffn[i]), bf(moe_router[c]), tm, f"route_{i}")
            h = _moe_post(*routed, p_all, moe_w_gu, moe_w_d, c, *ple, tm, i)
    return h.reshape(b, s, d)
```
